```python
import jax, jax.numpy as jnp
from jax import lax
import numpy as np

D_MODEL = 2048
BATCH = 1
SEQ = 16384
DEPTH = 1
DEC_BATCH = 16
DEC_SEQ = 32
PAST_LEN = 4096

CHUNK = 64
CONV_WIDTH = 1024
CONV_K = 3
N_HEADS = 16
N_KV_HEADS = 4
HEAD_DIM = 128
Q_PER_KV = N_HEADS // N_KV_HEADS
IDX_HEADS = 16
IDX_DIM = 64
TOPK_MAX = 256
N_MEM = 256
MEM_HEADS = 4
MEM_HEAD_DIM = 256
N_EXPERTS = 32
TOP_K = 4
D_FF = 2048
SWIGLU_LIMIT = 7.0
SWIGLU_ALPHA = 1.702
ROPE_THETA = 10000.0
EPS = 1e-6
Q_BLOCK = 128
ATTN_Q = N_HEADS * HEAD_DIM
ATTN_KV = N_KV_HEADS * HEAD_DIM
IDX_Q = IDX_HEADS * IDX_DIM
MEM_Q = MEM_HEADS * MEM_HEAD_DIM
COMB_SPLITS = (CONV_WIDTH, CONV_WIDTH, CONV_WIDTH, ATTN_Q, ATTN_KV, ATTN_KV,
               IDX_Q, IDX_DIM, IDX_HEADS, MEM_Q, D_MODEL, D_MODEL, D_MODEL)
D_COMB = 3 * CONV_WIDTH + ATTN_Q + 2 * ATTN_KV + IDX_Q + IDX_DIM + IDX_HEADS + MEM_Q + 3 * D_MODEL

kernel_name = 'hybrid_conv_dsa_memory_moe_stream_step'


def rmsnorm(x, g):
    xf = x.astype(jnp.float32)
    y = xf * lax.rsqrt(jnp.mean(xf * xf, axis=-1, keepdims=True) + EPS)
    return (y * g.astype(jnp.float32)).astype(x.dtype)


def rope(x, pos):
    half = x.shape[-1] // 2
    inv_freq = ROPE_THETA ** (-jnp.arange(half, dtype=jnp.float32) / half)
    ang = pos.astype(jnp.float32)[:, None] * inv_freq[None, :]
    cos = jnp.cos(ang)[None, :, None, :]
    sin = jnp.sin(ang)[None, :, None, :]
    xf = x.astype(jnp.float32)
    x1, x2 = xf[..., :half], xf[..., half:]
    return jnp.concatenate([x1 * cos - x2 * sin, x2 * cos + x1 * sin], axis=-1).astype(x.dtype)


def split_columns(z):
    offsets = []
    acc = 0
    for width in COMB_SPLITS[:-1]:
        acc += width
        offsets.append(acc)
    return jnp.split(z, offsets, axis=-1)


def short_conv(u, hist, w):
    T = u.shape[1]
    up = jnp.concatenate([hist, u], axis=1)
    y = w[0] * up[:, 0:T]
    for j in range(1, CONV_K):
        y = y + w[j] * up[:, j:j + T]
    return y, up[:, -(CONV_K - 1):]


def sparse_attention(q, k, v, qi, ki, wi, q_pos, topk):
    B, T = q.shape[0], q.shape[1]
    L = k.shape[1]
    qb = min(Q_BLOCK, T)
    nb = T // qb
    k_chunk = jnp.arange(L, dtype=jnp.int32) // CHUNK
    idx_scale = IDX_DIM ** -0.5
    attn_scale = HEAD_DIM ** -0.5
    gather_rows = jax.vmap(lambda rows, ix: rows[ix])

    def to_blocks(a):
        return jnp.moveaxis(a.reshape((B, nb, qb) + a.shape[2:]), 1, 0)

    def one_block(args):
        q_b, qi_b, w_b, pos_b = args
        rel = jax.nn.relu(jnp.einsum('bqjd,bsd->bqjs', qi_b, ki).astype(jnp.float32))
        score = jnp.einsum('bqjs,bqj->bqs', rel, w_b.astype(jnp.float32)) * idx_scale
        admissible = k_chunk[None, :] <= (pos_b // CHUNK)[:, None]
        score = jnp.where(admissible[None], score, -jnp.inf)
        top_vals, top_idx = lax.top_k(score, topk)
        valid = jnp.isfinite(top_vals)
        k_sel = gather_rows(k, top_idx)
        v_sel = gather_rows(v, top_idx)
        qg = q_b.reshape(B, qb, N_KV_HEADS, Q_PER_KV, HEAD_DIM)
        s = jnp.einsum('bqhgd,bqshd->bqhgs', qg, k_sel).astype(jnp.float32) * attn_scale
        s = jnp.where(valid[:, :, None, None, :], s, -jnp.inf)
        p = jax.nn.softmax(s, axis=-1).astype(v.dtype)
        o = jnp.einsum('bqhgs,bqshd->bqhgd', p, v_sel)
        return o.reshape(B, qb, N_HEADS, HEAD_DIM)

    out = lax.map(one_block, (to_blocks(q), to_blocks(qi), to_blocks(wi), q_pos.reshape(nb, qb)))
    return jnp.moveaxis(out, 0, 1).reshape(B, T, N_HEADS, HEAD_DIM)


def memory_attention(q, mk, mv):
    s = jnp.einsum('bthd,bmhd->bhtm', q, mk).astype(jnp.float32) * (MEM_HEAD_DIM ** -0.5)
    p = jax.nn.softmax(s, axis=-1).astype(mv.dtype)
    return jnp.einsum('bhtm,bmhd->bthd', p, mv)


def mixer_layer(x, pos, conv_hist, k_past, v_past, kidx_past, mem_k, mem_v, p):
    B, T = x.shape[0], x.shape[1]
    xn = rmsnorm(x, p['g_mix'])
    (c_in, c_b, c_c, q, k, v, qi, ki, wi, mq,
     gate_conv, gate_attn, gate_mem) = split_columns(xn @ p['w_comb'])
    conv_out, new_hist = short_conv(c_c * c_in, conv_hist, p['conv_w'])
    br_conv = (c_b * conv_out) @ p['w_conv_proj']
    q = rope(q.reshape(B, T, N_HEADS, HEAD_DIM), pos)
    k_new = rope(k.reshape(B, T, N_KV_HEADS, HEAD_DIM), pos)
    v_new = v.reshape(B, T, N_KV_HEADS, HEAD_DIM)
    qi = rope(qi.reshape(B, T, IDX_HEADS, IDX_DIM), pos)
    ki_new = rope(ki[:, :, None, :], pos)[:, :, 0, :]
    wi = wi * (IDX_HEADS ** -0.5)
    k_all = jnp.concatenate([k_past, k_new], axis=1)
    v_all = jnp.concatenate([v_past, v_new], axis=1)
    ki_all = jnp.concatenate([kidx_past, ki_new], axis=1)
    L = k_all.shape[1]
    attn = sparse_attention(q, k_all, v_all, qi, ki_all, wi, pos, min(TOPK_MAX, L // 4))
    br_attn = attn.reshape(B, T, ATTN_Q) @ p['w_attn_proj']
    mem_o = memory_attention(mq.reshape(B, T, MEM_HEADS, MEM_HEAD_DIM), mem_k, mem_v)
    br_mem = mem_o.reshape(B, T, MEM_Q) @ p['w_mem_proj']
    merged = (jax.nn.sigmoid(gate_conv) * br_conv
              + jax.nn.sigmoid(gate_attn) * br_attn
              + jax.nn.sigmoid(gate_mem) * br_mem)
    return x + merged @ p['w_merge_out'], new_hist, k_new, v_new, ki_new


def moe(xn, w_router, b_router, w_up_gate, b_up_gate, w_down, b_down):
    logits = (xn @ w_router).astype(jnp.float32) + b_router.astype(jnp.float32)
    top_vals, top_idx = lax.top_k(logits, TOP_K)
    probs = jax.nn.softmax(top_vals, axis=-1)
    combine = jnp.sum(jax.nn.one_hot(top_idx, N_EXPERTS, dtype=jnp.float32) * probs[..., None], axis=-2)

    def expert(acc, params):
        wug, bug, wd, bd, c = params
        h = xn @ wug + bug
        gate = jnp.minimum(h[:, :D_FF], SWIGLU_LIMIT)
        up = jnp.clip(h[:, D_FF:], -SWIGLU_LIMIT, SWIGLU_LIMIT)
        act = (up + 1) * gate * jax.nn.sigmoid(SWIGLU_ALPHA * gate)
        out = act @ wd + bd
        return acc + c[:, None].astype(out.dtype) * out, None

    out, _ = lax.scan(expert, jnp.zeros_like(xn), (w_up_gate, b_up_gate, w_down, b_down, combine.T))
    return out


def setup_inputs(seed: int = 0) -> dict:
    key = jax.random.key(seed)
    ks = jax.random.split(key, 26)
    f32 = jnp.float32

    def nrm(k, shape, scale):
        return jax.random.normal(k, shape, f32) * scale

    def gain(k, shape):
        return 1.0 + 0.01 * jax.random.normal(k, shape, f32)

    return {
        'x_prompt': nrm(ks[0], (BATCH, SEQ, D_MODEL), 1.0),
        'x_sample': nrm(ks[1], (DEC_BATCH, DEC_SEQ, D_MODEL), 1.0),
        'cache_attn_k': nrm(ks[2], (DEPTH, DEC_BATCH, PAST_LEN, N_KV_HEADS, HEAD_DIM), 1.0),
        'cache_attn_v': nrm(ks[3], (DEPTH, DEC_BATCH, PAST_LEN, N_KV_HEADS, HEAD_DIM), 1.0),
        'cache_idx_k': nrm(ks[4], (DEPTH, DEC_BATCH, PAST_LEN, IDX_DIM), 1.0),
        'state_conv': nrm(ks[5], (DEPTH, DEC_BATCH, CONV_K - 1, CONV_WIDTH), 1.0),
        'cache_mem_k': nrm(ks[6], (DEPTH, DEC_BATCH, N_MEM, MEM_HEADS, MEM_HEAD_DIM), 1.0),
        'cache_mem_v': nrm(ks[7], (DEPTH, DEC_BATCH, N_MEM, MEM_HEADS, MEM_HEAD_DIM), 1.0),
        'mem_prompt': nrm(ks[8], (BATCH, N_MEM, D_MODEL), 1.0),
        'g_mix': gain(ks[9], (DEPTH, D_MODEL)),
        'w_comb': nrm(ks[10], (DEPTH, D_MODEL, D_COMB), D_MODEL ** -0.5),
        'conv_w': nrm(ks[11], (DEPTH, CONV_K, CONV_WIDTH), CONV_K ** -0.5),
        'w_conv_proj': nrm(ks[12], (DEPTH, CONV_WIDTH, D_MODEL), CONV_WIDTH ** -0.5),
        'w_attn_proj': nrm(ks[13], (DEPTH, ATTN_Q, D_MODEL), ATTN_Q ** -0.5),
        'g_mem': gain(ks[14], (DEPTH, D_MODEL)),
        'w_mem_kv': nrm(ks[15], (DEPTH, D_MODEL, 2 * MEM_Q), D_MODEL ** -0.5),
        'w_mem_proj': nrm(ks[16], (DEPTH, MEM_Q, D_MODEL), MEM_Q ** -0.5),
        'w_merge_out': nrm(ks[17], (DEPTH, D_MODEL, D_MODEL), D_MODEL ** -0.5),
        'g_moe': gain(ks[18], (DEPTH, D_MODEL)),
        'w_router': nrm(ks[19], (DEPTH, D_MODEL, N_EXPERTS), D_MODEL ** -0.5),
        'b_router': nrm(ks[20], (DEPTH, N_EXPERTS), 0.01),
        'w_up_gate': nrm(ks[21], (DEPTH, N_EXPERTS, D_MODEL, 2 * D_FF), D_MODEL ** -0.5),
        'b_up_gate': nrm(ks[22], (DEPTH, N_EXPERTS, 2 * D_FF), 0.01),
        'w_down': nrm(ks[23], (DEPTH, N_EXPERTS, D_FF, D_MODEL), D_FF ** -0.5),
        'b_down': nrm(ks[24], (DEPTH, N_EXPERTS, D_MODEL), 0.01),
        'g_final': gain(ks[25], (D_MODEL,)),
    }


def reference(x_prompt, x_sample, cache_attn_k, cache_attn_v, cache_idx_k, state_conv,
              cache_mem_k, cache_mem_v, mem_prompt, g_mix, w_comb, conv_w, w_conv_proj,
              w_attn_proj, g_mem, w_mem_kv, w_mem_proj, w_merge_out, g_moe, w_router,
              b_router, w_up_gate, b_up_gate, w_down, b_down, g_final):
    B, T = x_prompt.shape[0], x_prompt.shape[1]
    Bs, Ts = x_sample.shape[0], x_sample.shape[1]
    past = cache_attn_k.shape[2]
    n_mem = mem_prompt.shape[1]
    pos_p = jnp.arange(T, dtype=jnp.int32)
    pos_s = past + jnp.arange(Ts, dtype=jnp.int32)
    xp, xs = x_prompt, x_sample
    k_p_l, v_p_l, ki_p_l, conv_p_l, mk_p_l, mv_p_l = [], [], [], [], [], []
    k_s_l, v_s_l, ki_s_l, conv_s_l = [], [], [], []
    for l in range(DEPTH):
        p = {'g_mix': g_mix[l], 'w_comb': w_comb[l], 'conv_w': conv_w[l],
             'w_conv_proj': w_conv_proj[l], 'w_attn_proj': w_attn_proj[l],
             'w_mem_proj': w_mem_proj[l], 'w_merge_out': w_merge_out[l]}
        mem_kv = rmsnorm(mem_prompt, g_mem[l]) @ w_mem_kv[l]
        mk_p = mem_kv[..., :MEM_Q].reshape(B, n_mem, MEM_HEADS, MEM_HEAD_DIM)
        mv_p = mem_kv[..., MEM_Q:].reshape(B, n_mem, MEM_HEADS, MEM_HEAD_DIM)
        empty_kv = jnp.zeros((B, 0, N_KV_HEADS, HEAD_DIM), xp.dtype)
        empty_idx = jnp.zeros((B, 0, IDX_DIM), xp.dtype)
        zero_hist = jnp.zeros((B, CONV_K - 1, CONV_WIDTH), xp.dtype)
        xp, hist_p, k_p, v_p, ki_p = mixer_layer(xp, pos_p, zero_hist, empty_kv, empty_kv,
                                                 empty_idx, mk_p, mv_p, p)
        xs, hist_s, k_s, v_s, ki_s = mixer_layer(xs, pos_s, state_conv[l], cache_attn_k[l],
                                                 cache_attn_v[l], cache_idx_k[l],
                                                 cache_mem_k[l], cache_mem_v[l], p)
        flat = jnp.concatenate([xp.reshape(B * T, D_MODEL), xs.reshape(Bs * Ts, D_MODEL)], axis=0)
        flat = flat + moe(rmsnorm(flat, g_moe[l]), w_router[l], b_router[l], w_up_gate[l],
                          b_up_gate[l], w_down[l], b_down[l])
        xp = flat[:B * T].reshape(B, T, D_MODEL)
        xs = flat[B * T:].reshape(Bs, Ts, D_MODEL)
        k_p_l.append(k_p); v_p_l.append(v_p); ki_p_l.append(ki_p); conv_p_l.append(hist_p)
        mk_p_l.append(mk_p); mv_p_l.append(mv_p)
        k_s_l.append(k_s); v_s_l.append(v_s); ki_s_l.append(ki_s); conv_s_l.append(hist_s)
    y_prompt = rmsnorm(xp, g_final)
    y_sample = rmsnorm(xs, g_final)
    return (y_prompt, y_sample,
            jnp.stack(k_p_l), jnp.stack(v_p_l), jnp.stack(ki_p_l), jnp.stack(conv_p_l),
            jnp.stack(mk_p_l), jnp.stack(mv_p_l),
            jnp.stack(k_s_l), jnp.stack(v_s_l), jnp.stack(ki_s_l), jnp.stack(conv_s_l))
```

```python
import functools

import jax
import jax.numpy as jnp
from jax import lax
from jax.experimental import pallas as pl
from jax.experimental.pallas import tpu as pltpu

F32 = jnp.float32
BF16 = jnp.bfloat16
I32 = jnp.int32

CHUNK = 64
IDX_HEADS = 16
TOPK_MAX = 256
TOP_K = 4
SWIGLU_LIMIT = 7.0
SWIGLU_ALPHA = 1.702
ROPE_THETA = 10000.0
EPS = 1e-6

LANES = 128
VMEM_LIMIT = 56 * 1024 * 1024
MOE_TILE = 1024
MASK_NEG = -1e30
INT_MIN = -2147483648
NEGINF_KEY = -2139095041


def _cparams(sem):
    return pltpu.CompilerParams(dimension_semantics=sem, vmem_limit_bytes=VMEM_LIMIT)


def _pick(n, pref):
    if n <= pref:
        return n
    t = pref
    while n % t:
        t //= 2
    return t


def _rms_kernel(x_ref, g_ref, o_ref):
    x = x_ref[...]
    ms = jnp.mean(x * x, axis=-1, keepdims=True)
    o_ref[...] = (x * lax.rsqrt(ms + EPS) * g_ref[...]).astype(o_ref.dtype)


def _rmsnorm(x, g, out_dtype):
    n, d = x.shape
    tm = _pick(n, 512)
    return pl.pallas_call(
        _rms_kernel,
        grid=(n // tm,),
        in_specs=[pl.BlockSpec((tm, d), lambda i: (i, 0)),
                  pl.BlockSpec((1, d), lambda i: (0, 0))],
        out_specs=pl.BlockSpec((tm, d), lambda i: (i, 0)),
        out_shape=jax.ShapeDtypeStruct((n, d), out_dtype),
        compiler_params=_cparams(("parallel",)),
        name="rmsnorm",
    )(x, g.reshape(1, d))


def _mm_kernel(epilogue, n_w, n_extra, x_ref, *refs):
    w_refs = refs[:n_w]
    extras = refs[n_w:n_w + n_extra]
    outs = refs[n_w + n_extra:]
    x = x_ref[...]
    accs = [jnp.dot(x, w[...], preferred_element_type=F32) for w in w_refs]
    epilogue(accs, extras, outs)


def _mm(x, ws, epilogue, outs, extras=(), *, tn, tm=512, name="mm"):
    n, k = x.shape
    tm = _pick(n, tm)
    ncol = outs[0][0] // outs[0][2]
    in_specs = [pl.BlockSpec((tm, k), lambda i, j: (i, 0))]
    args = [x]
    for w, off in ws:
        in_specs.append(pl.BlockSpec((k, tn), functools.partial(lambda i, j, o: (0, o + j), o=off)))
        args.append(w)
    for a, bs, im in extras:
        in_specs.append(pl.BlockSpec(bs, im))
        args.append(a)
    out_specs, out_shape = [], []
    for width, dt, bw, im in outs:
        out_specs.append(pl.BlockSpec((tm, bw), im if im is not None else (lambda i, j: (i, j))))
        out_shape.append(jax.ShapeDtypeStruct((n, width), dt))
    res = pl.pallas_call(
        functools.partial(_mm_kernel, epilogue, len(ws), len(extras)),
        grid=(n // tm, ncol),
        in_specs=in_specs,
        out_specs=out_specs,
        out_shape=out_shape,
        compiler_params=_cparams(("parallel", "arbitrary")),
        name=name,
    )(*args)
    return res


def _rope_groups(acc, cos, sin, half):
    tn = acc.shape[1]
    lane = lax.broadcasted_iota(I32, (acc.shape[0], LANES), 1)
    pieces = []
    for c in range(tn // LANES):
        a = acc[:, c * LANES:(c + 1) * LANES]
        if 2 * half == LANES:
            partner = pltpu.roll(a, half, axis=1)
        else:
            first = (lane & (2 * half - 1)) < half
            partner = jnp.where(first, pltpu.roll(a, LANES - half, axis=1), pltpu.roll(a, half, axis=1))
        pieces.append(a * cos + partner * sin)
    return pieces


def _conv_kernel(u_ref, cb_ref, h_ref, w_ref, o_ref, nh_ref, carry_ref):
    t = pl.program_id(1)

    @pl.when(t == 0)
    def _():
        carry_ref[0:2, :] = h_ref[...]

    u = u_ref[...]
    tm = u.shape[0]
    h0 = carry_ref[0:1, :]
    h1 = carry_ref[1:2, :]
    row = lax.broadcasted_iota(I32, u.shape, 0)
    p1 = jnp.where(row == 0, h1, pltpu.roll(u, 1, axis=0))
    p2 = jnp.where(row == 0, h0, jnp.where(row == 1, h1, pltpu.roll(u, 2, axis=0)))
    w = w_ref[...]
    y = w[0:1, :] * p2 + w[1:2, :] * p1 + w[2:3, :] * u
    o_ref[...] = (cb_ref[...] * y).astype(o_ref.dtype)
    last = u[tm - 2:tm, :]
    carry_ref[0:2, :] = last
    nh_ref[...] = last


def _short_conv(u, cb, hist, conv_w, row_off, nb, t_len):
    c = u.shape[1]
    tm = _pick(t_len, 512)
    nt = t_len // tm
    off = row_off // tm
    rows = lambda b, t: (off + b * nt + t, 0)
    conv_pre, new_hist = pl.pallas_call(
        _conv_kernel,
        grid=(nb, nt),
        in_specs=[pl.BlockSpec((tm, c), rows),
                  pl.BlockSpec((tm, c), rows),
                  pl.BlockSpec((None, 2, c), lambda b, t: (b, 0, 0)),
                  pl.BlockSpec((3, c), lambda b, t: (0, 0))],
        out_specs=[pl.BlockSpec((tm, c), lambda b, t: (b * nt + t, 0)),
                   pl.BlockSpec((None, 2, c), lambda b, t: (b, 0, 0))],
        out_shape=[jax.ShapeDtypeStruct((nb * t_len, c), BF16),
                   jax.ShapeDtypeStruct((nb, 2, c), F32)],
        scratch_shapes=[pltpu.VMEM((8, c), F32)],
        compiler_params=_cparams(("arbitrary", "arbitrary")),
        name="short_conv",
    )(u, cb, hist, conv_w)
    return conv_pre, new_hist


def _sel_kernel(qi_ref, wi_ref, kit_ref, bias_ref, key_ref, pcut_ref, *,
                tq, ch, lpad, l_valid, causal, ktop, idx_dim):
    i = pl.program_id(1)
    nch_all = lpad // ch
    if causal:
        nch = jnp.minimum(((i + 1) * tq + ch - 1) // ch, nch_all)
    else:
        nch = nch_all
    scale = idx_dim ** -0.5
    qi = qi_ref[...]
    w = wi_ref[...]
    qh = [qi[:, h * idx_dim:(h + 1) * idx_dim] for h in range(IDX_HEADS)]
    row_chunk = (i * tq + lax.broadcasted_iota(I32, (tq, ch), 0)) // CHUNK
    col_iota = lax.broadcasted_iota(I32, (tq, ch), 1)

    def score_chunk(c, carry):
        c0 = pl.multiple_of(c * ch, ch)
        kt = kit_ref[:, pl.ds(c0, ch)]
        acc = jnp.zeros((tq, ch), F32)
        for h in range(IDX_HEADS):
            d = jnp.dot(qh[h], kt, preferred_element_type=F32)
            acc = acc + jnp.maximum(d, 0.0) * w[:, h:h + 1]
        s = acc * scale
        col = c0 + col_iota
        adm = col < l_valid
        if causal:
            adm = jnp.logical_and(adm, (col // CHUNK) <= row_chunk)
        bits = pltpu.bitcast(s, I32)
        key = jnp.where(bits < 0, bits ^ 0x7FFFFFFF, bits)
        key_ref[:, pl.ds(c0, ch)] = jnp.where(adm, key, NEGINF_KEY)
        return carry

    lax.fori_loop(0, nch, score_chunk, 0)

    def fold(m):
        out = m[:, 0:LANES]
        for s in range(1, ch // LANES):
            out = out + m[:, s * LANES:(s + 1) * LANES]
        return out

    def count(pred):
        def body(c, cnt):
            c0 = pl.multiple_of(c * ch, ch)
            blk = key_ref[:, pl.ds(c0, ch)]
            return cnt + fold(jnp.where(pred(blk, c0), 1.0, 0.0))
        cnt = lax.fori_loop(0, nch, body, jnp.zeros((tq, LANES), F32))
        return jnp.sum(cnt, axis=-1, keepdims=True)

    kf = float(ktop)
    zero = jnp.zeros((tq, 1), I32)
    cand0 = jnp.where(count(lambda blk, c0: blk >= zero) >= kf, 0, INT_MIN).astype(I32)

    def bit_body(it, cand):
        trial = cand | lax.shift_left(jnp.int32(1), 30 - it)
        c = count(lambda blk, c0: blk >= trial)
        return jnp.where(c >= kf, trial, cand)

    thr = lax.fori_loop(0, 31, bit_body, cand0)
    all_adm = thr <= NEGINF_KEY
    thr = jnp.where(all_adm, NEGINF_KEY, thr)
    c_gt = count(lambda blk, c0: blk > thr)
    c_eq = count(lambda blk, c0: blk == thr)
    need = kf - c_gt
    pcut_ref[...] = jnp.where(all_adm, 0, lpad).astype(I32)
    tie = jnp.logical_and(jnp.logical_not(all_adm), c_eq > need)

    @pl.when(jnp.max(jnp.where(tie, 1.0, 0.0)) > 0.0)
    def _():
        nbits = max(1, (lpad - 1).bit_length())

        def pbody(it, q):
            trial = q | lax.shift_left(jnp.int32(1), nbits - 1 - it)
            c = count(lambda blk, c0: jnp.logical_and(blk == thr, (c0 + col_iota) < trial))
            return jnp.where(c < need, trial, q)

        q = lax.fori_loop(0, nbits, pbody, jnp.zeros((tq, 1), I32))
        pcut_ref[...] = jnp.where(tie, q + 1, pcut_ref[...])

    pcut = pcut_ref[...]

    def write_chunk(c, carry):
        c0 = pl.multiple_of(c * ch, ch)
        blk = key_ref[:, pl.ds(c0, ch)]
        sel = jnp.logical_or(blk > thr, jnp.logical_and(blk == thr, (c0 + col_iota) < pcut))
        bias_ref[:, pl.ds(c0, ch)] = jnp.where(sel, 0.0, MASK_NEG).astype(bias_ref.dtype)
        return carry

    lax.fori_loop(0, nch, write_chunk, 0)

    def fill_chunk(c, carry):
        c0 = pl.multiple_of(c * ch, ch)
        bias_ref[:, pl.ds(c0, ch)] = jnp.full((tq, ch), MASK_NEG, bias_ref.dtype)
        return carry

    lax.fori_loop(nch, nch_all, fill_chunk, 0)


def _select(qi, wi, kit, *, row_off, nb, t_len, l_valid, causal, ktop, idx_dim):
    lpad = kit.shape[2]
    tq = _pick(t_len, 128)
    ch = _pick(lpad, 512)
    assert ch >= ktop and lpad % ch == 0
    nt = t_len // tq
    off = row_off // tq
    rows = lambda b, i: (off + b * nt + i, 0)
    kern = functools.partial(_sel_kernel, tq=tq, ch=ch, lpad=lpad, l_valid=l_valid,
                             causal=causal, ktop=ktop, idx_dim=idx_dim)
    return pl.pallas_call(
        kern,
        grid=(nb, nt),
        in_specs=[pl.BlockSpec((tq, qi.shape[1]), rows),
                  pl.BlockSpec((tq, wi.shape[1]), rows),
                  pl.BlockSpec((None, idx_dim, lpad), lambda b, i: (b, 0, 0))],
        out_specs=pl.BlockSpec((None, tq, lpad), lambda b, i: (b, i, 0)),
        out_shape=jax.ShapeDtypeStruct((nb, t_len, lpad), BF16),
        scratch_shapes=[pltpu.VMEM((tq, lpad), I32), pltpu.VMEM((tq, 1), I32)],
        compiler_params=_cparams(("parallel", "arbitrary")),
        name="index_select",
    )(qi, wi, kit)


def _att_kernel(q_ref, k_ref, v_ref, b_ref, o_ref, m_ref, l_ref, acc_ref, *,
                tq, tk, causal, n_heads, n_kv, hd):
    i = pl.program_id(1)
    j = pl.program_id(2)
    nj = pl.num_programs(2)
    rep = n_heads // n_kv

    @pl.when(j == 0)
    def _():
        m_ref[...] = jnp.full(m_ref.shape, MASK_NEG, F32)
        l_ref[...] = jnp.zeros(l_ref.shape, F32)
        acc_ref[...] = jnp.zeros(acc_ref.shape, F32)

    def compute():
        bias = b_ref[...].astype(F32)
        for h in range(n_heads):
            g = h // rep
            q = q_ref[:, h * hd:(h + 1) * hd]
            k = k_ref[:, g * hd:(g + 1) * hd]
            v = v_ref[:, g * hd:(g + 1) * hd]
            s = lax.dot_general(q, k, (((1,), (1,)), ((), ())), preferred_element_type=F32) + bias
            m_prev = m_ref[h]
            m_new = jnp.maximum(m_prev, jnp.max(s, axis=-1, keepdims=True))
            alpha = jnp.exp(m_prev - m_new)
            p = jnp.exp(s - m_new)
            l_ref[h] = alpha * l_ref[h] + jnp.sum(p, axis=-1, keepdims=True)
            pv = jnp.dot(p.astype(BF16), v, preferred_element_type=F32)
            acc_ref[:, h * hd:(h + 1) * hd] = alpha * acc_ref[:, h * hd:(h + 1) * hd] + pv
            m_ref[h] = m_new

    if causal:
        pl.when(j * tk < (i + 1) * tq)(compute)
    else:
        compute()

    @pl.when(j == nj - 1)
    def _():
        for h in range(n_heads):
            o_ref[:, h * hd:(h + 1) * hd] = (acc_ref[:, h * hd:(h + 1) * hd] / l_ref[h]).astype(o_ref.dtype)


def _attention(q, k, v, bias, *, row_off, nb, t_len, causal, n_heads, n_kv, hd):
    lpad = k.shape[1]
    tq = _pick(t_len, 256)
    tk = _pick(lpad, 512)
    nt = t_len // tq
    nk = lpad // tk
    off = row_off // tq
    if causal:
        kmap = lambda b, i, j: (b, jnp.minimum(j, ((i + 1) * tq - 1) // tk), 0)
        bmap = lambda b, i, j: (b, i, jnp.minimum(j, ((i + 1) * tq - 1) // tk))
    else:
        kmap = lambda b, i, j: (b, j, 0)
        bmap = lambda b, i, j: (b, i, j)
    kern = functools.partial(_att_kernel, tq=tq, tk=tk, causal=causal, n_heads=n_heads, n_kv=n_kv, hd=hd)
    return pl.pallas_call(
        kern,
        grid=(nb, nt, nk),
        in_specs=[pl.BlockSpec((tq, n_heads * hd), lambda b, i, j: (off + b * nt + i, 0)),
                  pl.BlockSpec((None, tk, n_kv * hd), kmap),
                  pl.BlockSpec((None, tk, n_kv * hd), kmap),
                  pl.BlockSpec((None, tq, tk), bmap)],
        out_specs=pl.BlockSpec((tq, n_heads * hd), lambda b, i, j: (b * nt + i, 0)),
        out_shape=jax.ShapeDtypeStruct((nb * t_len, n_heads * hd), BF16),
        scratch_shapes=[pltpu.VMEM((n_heads, tq, 1), F32),
                        pltpu.VMEM((n_heads, tq, 1), F32),
                        pltpu.VMEM((tq, n_heads * hd), F32)],
        compiler_params=_cparams(("parallel", "parallel", "arbitrary")),
        name="sparse_attention",
    )(q, k, v, bias)


def _mematt_kernel(q_ref, k_ref, v_ref, o_ref, *, n_heads, hd):
    scale = hd ** -0.5
    for h in range(n_heads):
        q = q_ref[:, h * hd:(h + 1) * hd]
        k = k_ref[:, h * hd:(h + 1) * hd].astype(BF16)
        v = v_ref[:, h * hd:(h + 1) * hd].astype(BF16)
        s = lax.dot_general(q, k, (((1,), (1,)), ((), ())), preferred_element_type=F32) * scale
        m = jnp.max(s, axis=-1, keepdims=True)
        p = jnp.exp(s - m)
        l = jnp.sum(p, axis=-1, keepdims=True)
        o = jnp.dot(p.astype(BF16), v, preferred_element_type=F32) / l
        o_ref[:, h * hd:(h + 1) * hd] = o.astype(o_ref.dtype)


def _mem_attention(mq, mk, mv, *, row_off, nb, t_len, n_heads, hd):
    tq = _pick(t_len, 512)
    nt = t_len // tq
    off = row_off // tq
    n_mem = mk.shape[1]
    w = n_heads * hd
    return pl.pallas_call(
        functools.partial(_mematt_kernel, n_heads=n_heads, hd=hd),
        grid=(nb, nt),
        in_specs=[pl.BlockSpec((tq, w), lambda b, i: (off + b * nt + i, 0)),
                  pl.BlockSpec((None, n_mem, w), lambda b, i: (b, 0, 0)),
                  pl.BlockSpec((None, n_mem, w), lambda b, i: (b, 0, 0))],
        out_specs=pl.BlockSpec((tq, w), lambda b, i: (b * nt + i, 0)),
        out_shape=jax.ShapeDtypeStruct((nb * t_len, w), BF16),
        compiler_params=_cparams(("parallel", "parallel")),
        name="memory_attention",
    )(mq, mk, mv)


def _router_kernel(x_ref, g_ref, w_ref, b_ref, xn_ref, idx_ref, p_ref):
    x = x_ref[...]
    ms = jnp.mean(x * x, axis=-1, keepdims=True)
    xn = x * lax.rsqrt(ms + EPS) * g_ref[...]
    xn_ref[...] = xn
    logits = jnp.dot(xn.astype(BF16), w_ref[...], preferred_element_type=F32) + b_ref[...]
    lane = lax.broadcasted_iota(I32, logits.shape, 1)
    lane_f = lane.astype(F32)
    vals, idxs = [], []
    l = logits
    for _ in range(TOP_K):
        m = jnp.max(l, axis=-1, keepdims=True)
        ix = jnp.min(jnp.where(l == m, lane_f, float(LANES)), axis=-1, keepdims=True)
        vals.append(m)
        idxs.append(ix)
        l = jnp.where(lane_f == ix, -jnp.inf, l)
    es = [jnp.exp(v - vals[0]) for v in vals]
    den = es[0]
    for e in es[1:]:
        den = den + e
    idx_out = jnp.zeros(logits.shape, F32)
    p_out = jnp.zeros(logits.shape, F32)
    for r in range(TOP_K):
        idx_out = jnp.where(lane == r, idxs[r], idx_out)
        p_out = jnp.where(lane == r, es[r] / den, p_out)
    idx_ref[...] = idx_out.astype(I32)
    p_ref[...] = p_out


def _router(x1, g, w_router, b_router):
    n, d = x1.shape
    ne = w_router.shape[1]
    tm = _pick(n, 512)
    w_pad = jnp.zeros((d, LANES), BF16).at[:, :ne].set(w_router.astype(BF16))
    b_pad = jnp.full((1, LANES), MASK_NEG, F32).at[0, :ne].set(b_router)
    return pl.pallas_call(
        _router_kernel,
        grid=(n // tm,),
        in_specs=[pl.BlockSpec((tm, d), lambda i: (i, 0)),
                  pl.BlockSpec((1, d), lambda i: (0, 0)),
                  pl.BlockSpec((d, LANES), lambda i: (0, 0)),
                  pl.BlockSpec((1, LANES), lambda i: (0, 0))],
        out_specs=[pl.BlockSpec((tm, d), lambda i: (i, 0)),
                   pl.BlockSpec((tm, LANES), lambda i: (i, 0)),
                   pl.BlockSpec((tm, LANES), lambda i: (i, 0))],
        out_shape=[jax.ShapeDtypeStruct((n, d), F32),
                   jax.ShapeDtypeStruct((n, LANES), I32),
                   jax.ShapeDtypeStruct((n, LANES), F32)],
        compiler_params=_cparams(("parallel",)),
        name="router",
    )(x1, g.reshape(1, d), w_pad, b_pad)


def _row_copy(src_hbm, dst, src_row, dst_row, sem):
    return pltpu.make_async_copy(src_hbm.at[pl.ds(src_row, 1)], dst.at[pl.ds(dst_row, 1)], sem)


def _moe_kernel(te_ref, nu_ref, tok_ref, x_hbm, wg_ref, wu_ref, bg_ref, bu_ref, wd_ref, bd_ref,
                o_ref, xf_ref, xb_ref, sem, *, tm):
    t = pl.program_id(0)
    f = pl.program_id(1)

    @pl.when(t < nu_ref[0])
    def _():
        @pl.when(f == 0)
        def _():
            def issue(r, c):
                _row_copy(x_hbm, xf_ref, tok_ref[0, 0, r], r, sem).start()
                return c

            lax.fori_loop(0, tm, issue, 0)

            def wait(r, c):
                _row_copy(x_hbm, xf_ref, tok_ref[0, 0, r], r, sem).wait()
                return c

            lax.fori_loop(0, tm, wait, 0)
            xb_ref[...] = xf_ref[...].astype(BF16)

        xb = xb_ref[...]
        hg = jnp.dot(xb, wg_ref[...].astype(BF16), preferred_element_type=F32) + bg_ref[...]
        hu = jnp.dot(xb, wu_ref[...].astype(BF16), preferred_element_type=F32) + bu_ref[...]
        gate = jnp.minimum(hg, SWIGLU_LIMIT)
        up = jnp.clip(hu, -SWIGLU_LIMIT, SWIGLU_LIMIT)
        act = (up + 1.0) * gate * jax.nn.sigmoid(SWIGLU_ALPHA * gate)
        contrib = jnp.dot(act.astype(BF16), wd_ref[...].astype(BF16), preferred_element_type=F32)

        @pl.when(f == 0)
        def _():
            o_ref[...] = contrib + bd_ref[...]

        @pl.when(f > 0)
        def _():
            o_ref[...] += contrib

    @pl.when(jnp.logical_and(t >= nu_ref[0], f == 0))
    def _():
        o_ref[...] = jnp.zeros(o_ref.shape, o_ref.dtype)


def _moe_experts(xn, tile_expert, n_used, slot_token, w_up_gate, b_up_gate, w_down, b_down, *, tm, fc):
    n, d = xn.shape
    ne, _, ff2 = w_up_gate.shape
    ff = ff2 // 2
    nf = ff // fc
    mt = slot_token.shape[0]

    def fe(t, f, nu):
        return jnp.where(t < nu[0], f, nf - 1)

    def tl(t, nu):
        return jnp.minimum(t, nu[0] - 1)

    grid_spec = pltpu.PrefetchScalarGridSpec(
        num_scalar_prefetch=2,
        grid=(mt, nf),
        in_specs=[
            pl.BlockSpec((1, 1, tm), lambda t, f, te, nu: (tl(t, nu), 0, 0), memory_space=pltpu.SMEM),
            pl.BlockSpec(memory_space=pl.ANY),
            pl.BlockSpec((None, d, fc), lambda t, f, te, nu: (te[t], 0, fe(t, f, nu))),
            pl.BlockSpec((None, d, fc), lambda t, f, te, nu: (te[t], 0, nf + fe(t, f, nu))),
            pl.BlockSpec((None, 1, fc), lambda t, f, te, nu: (te[t], 0, fe(t, f, nu))),
            pl.BlockSpec((None, 1, fc), lambda t, f, te, nu: (te[t], 0, nf + fe(t, f, nu))),
            pl.BlockSpec((None, fc, d), lambda t, f, te, nu: (te[t], fe(t, f, nu), 0)),
            pl.BlockSpec((None, 1, d), lambda t, f, te, nu: (te[t], 0, 0)),
        ],
        out_specs=pl.BlockSpec((tm, d), lambda t, f, te, nu: (t, 0)),
        scratch_shapes=[pltpu.VMEM((tm, d), F32), pltpu.VMEM((tm, d), BF16), pltpu.SemaphoreType.DMA(())],
    )
    return pl.pallas_call(
        functools.partial(_moe_kernel, tm=tm),
        grid_spec=grid_spec,
        out_shape=jax.ShapeDtypeStruct((mt * tm, d), F32),
        compiler_params=_cparams(("arbitrary", "arbitrary")),
        name="moe_experts",
    )(tile_expert, n_used, slot_token, xn, w_up_gate, w_up_gate,
      b_up_gate.reshape(ne, 1, ff2), b_up_gate.reshape(ne, 1, ff2), w_down, b_down.reshape(ne, 1, d))


def _combine_kernel(slot_ref, x_ref, p_ref, g_ref, y_hbm, o_ref, ybuf_ref, sem, *, tm):
    def issue(r, c):
        for k in range(TOP_K):
            _row_copy(y_hbm, ybuf_ref, slot_ref[0, 0, r * TOP_K + k], k * tm + r, sem).start()
        return c

    lax.fori_loop(0, tm, issue, 0)

    def wait(r, c):
        for k in range(TOP_K):
            _row_copy(y_hbm, ybuf_ref, slot_ref[0, 0, r * TOP_K + k], k * tm + r, sem).wait()
        return c

    lax.fori_loop(0, tm, wait, 0)
    p = p_ref[...]
    x = x_ref[...]
    for k in range(TOP_K):
        x = x + p[:, k:k + 1] * ybuf_ref[k * tm:(k + 1) * tm, :]
    ms = jnp.mean(x * x, axis=-1, keepdims=True)
    o_ref[...] = x * lax.rsqrt(ms + EPS) * g_ref[...]


def _combine(x1, probs, pos_slot, y, g_final, *, tm):
    n, d = x1.shape
    nt = n // tm
    return pl.pallas_call(
        functools.partial(_combine_kernel, tm=tm),
        grid=(nt,),
        in_specs=[pl.BlockSpec((1, 1, TOP_K * tm), lambda i: (i, 0, 0), memory_space=pltpu.SMEM),
                  pl.BlockSpec((tm, d), lambda i: (i, 0)),
                  pl.BlockSpec((tm, LANES), lambda i: (i, 0)),
                  pl.BlockSpec((1, d), lambda i: (0, 0)),
                  pl.BlockSpec(memory_space=pl.ANY)],
        out_specs=pl.BlockSpec((tm, d), lambda i: (i, 0)),
        out_shape=jax.ShapeDtypeStruct((n, d), F32),
        scratch_shapes=[pltpu.VMEM((TOP_K * tm, d), F32), pltpu.SemaphoreType.DMA(())],
        compiler_params=_cparams(("arbitrary",)),
        name="moe_combine",
    )(pos_slot.reshape(nt, 1, TOP_K * tm), x1, probs, g_final.reshape(1, d), y)


def _rope_tables(pos, dim):
    half = dim // 2
    inv_freq = ROPE_THETA ** (-jnp.arange(half, dtype=F32) / half)
    ang = pos.astype(F32)[:, None] * inv_freq[None, :]
    cos, sin = jnp.cos(ang), jnp.sin(ang)
    reps = LANES // dim
    cos_t = jnp.tile(jnp.concatenate([cos, cos], axis=-1), (1, reps))
    sin_t = jnp.tile(jnp.concatenate([-sin, sin], axis=-1), (1, reps))
    return cos_t, sin_t


def _pad_len(l, mult):
    return (l + mult - 1) // mult * mult


def kernel(x_prompt, x_sample, cache_attn_k, cache_attn_v, cache_idx_k, state_conv, cache_mem_k, cache_mem_v, mem_prompt, g_mix, w_comb, conv_w, w_conv_proj, w_attn_proj, g_mem, w_mem_kv, w_mem_proj, w_merge_out, g_moe, w_router, b_router, w_up_gate, b_up_gate, w_down, b_down, g_final):
    bp, t_p, d = x_prompt.shape
    bs, t_s, _ = x_sample.shape
    assert bp == 1 and g_mix.shape[0] == 1
    past = cache_attn_k.shape[2]
    n_kv, hd = cache_attn_k.shape[3], cache_attn_k.shape[4]
    idx_dim = cache_idx_k.shape[3]
    cw = conv_w.shape[2]
    attn_q = w_attn_proj.shape[1]
    n_heads = attn_q // hd
    attn_kv = n_kv * hd
    idx_q = IDX_HEADS * idx_dim
    n_mem, mem_heads, mem_hd = cache_mem_k.shape[2], cache_mem_k.shape[3], cache_mem_k.shape[4]
    mem_q = mem_heads * mem_hd
    n_p, n_s = bp * t_p, bs * t_s
    n = n_p + n_s

    o_cin, o_cb, o_cc = 0, cw, 2 * cw
    o_q = 3 * cw
    o_k = o_q + attn_q
    o_v = o_k + attn_kv
    o_qi = o_v + attn_kv
    o_ki = o_qi + idx_q
    o_wi = o_ki + idx_dim
    o_mq = o_wi + IDX_HEADS
    o_g = o_mq + mem_q
    wc = w_comb[0]
    w_main = wc[:, :o_ki].astype(BF16)
    w_kw = jnp.zeros((d, LANES), BF16).at[:, :idx_dim + IDX_HEADS].set(wc[:, o_ki:o_mq].astype(BF16))
    w_mq = wc[:, o_mq:o_g].astype(BF16)
    w_gates = wc[:, o_g:].astype(BF16)

    x = jnp.concatenate([x_prompt.reshape(n_p, d), x_sample.reshape(n_s, d)], axis=0)
    pos = jnp.concatenate([jnp.arange(t_p, dtype=I32),
                           jnp.tile(past + jnp.arange(t_s, dtype=I32), bs)])
    cos_h, sin_h = _rope_tables(pos, hd)
    cos_i, sin_i = _rope_tables(pos, idx_dim)
    tm = _pick(n, 512)
    rowtab = lambda a: (a, (tm, LANES), lambda i, j: (i, 0))

    xn = _rmsnorm(x, g_mix[0], BF16)

    tn_c = _pick(cw, 512)
    nbc = cw // tn_c

    def ep_conv(accs, extras, outs):
        outs[0][...] = accs[2] * accs[0]
        outs[1][...] = accs[1]

    u, cb = _mm(xn, [(w_main, 0), (w_main, nbc), (w_main, 2 * nbc)], ep_conv,
                [(cw, F32, tn_c, None), (cw, F32, tn_c, None)], tn=tn_c, name="proj_conv")

    attn_scale = hd ** -0.5
    tn_q = _pick(attn_q, 1024)

    def ep_q(accs, extras, outs):
        pieces = _rope_groups(accs[0], extras[0][...], extras[1][...], hd // 2)
        for c, p in enumerate(pieces):
            outs[0][:, c * LANES:(c + 1) * LANES] = (p * attn_scale).astype(BF16)

    (q_r,) = _mm(xn, [(w_main, o_q // tn_q)], ep_q, [(attn_q, BF16, tn_q, None)],
                 extras=[rowtab(cos_h), rowtab(sin_h)], tn=tn_q, name="proj_q")

    def ep_k(accs, extras, outs):
        pieces = _rope_groups(accs[0], extras[0][...], extras[1][...], hd // 2)
        for c, p in enumerate(pieces):
            outs[0][:, c * LANES:(c + 1) * LANES] = p
            outs[1][:, c * LANES:(c + 1) * LANES] = p.astype(BF16)

    k_f, k_b = _mm(xn, [(w_main, o_k // attn_kv)], ep_k,
                   [(attn_kv, F32, attn_kv, None), (attn_kv, BF16, attn_kv, None)],
                   extras=[rowtab(cos_h), rowtab(sin_h)], tn=attn_kv, name="proj_k")

    def ep_v(accs, extras, outs):
        outs[0][...] = accs[0]
        outs[1][...] = accs[0].astype(BF16)

    v_f, v_b = _mm(xn, [(w_main, o_v // attn_kv)], ep_v,
                   [(attn_kv, F32, attn_kv, None), (attn_kv, BF16, attn_kv, None)],
                   tn=attn_kv, name="proj_v")

    tn_i = _pick(idx_q, 1024)

    def ep_qi(accs, extras, outs):
        pieces = _rope_groups(accs[0], extras[0][...], extras[1][...], idx_dim // 2)
        for c, p in enumerate(pieces):
            outs[0][:, c * LANES:(c + 1) * LANES] = p.astype(BF16)

    (qi_r,) = _mm(xn, [(w_main, o_qi // tn_i)], ep_qi, [(idx_q, BF16, tn_i, None)],
                  extras=[rowtab(cos_i), rowtab(sin_i)], tn=tn_i, name="proj_qi")

    wi_scale = IDX_HEADS ** -0.5

    def ep_kw(accs, extras, outs):
        a = accs[0]
        (roped,) = _rope_groups(a, extras[0][...], extras[1][...], idx_dim // 2)
        lane = lax.broadcasted_iota(I32, a.shape, 1)
        outs[0][...] = jnp.where(lane < idx_dim, roped, a * wi_scale)

    (kw,) = _mm(xn, [(w_kw, 0)], ep_kw, [(LANES, F32, LANES, None)],
                extras=[rowtab(cos_i), rowtab(sin_i)], tn=LANES, name="proj_ki_wi")
    ki_f = kw[:, :idx_dim]
    wi = kw[:, idx_dim:idx_dim + IDX_HEADS]

    def ep_cast(accs, extras, outs):
        outs[0][...] = accs[0].astype(outs[0].dtype)

    tn_m = _pick(mem_q, 1024)
    (mq,) = _mm(xn, [(w_mq, 0)], ep_cast, [(mem_q, BF16, tn_m, None)], tn=tn_m, name="proj_mq")

    def ep_sig(accs, extras, outs):
        outs[0][...] = jax.nn.sigmoid(accs[0])

    tn_g = _pick(d, 1024)
    (sg,) = _mm(xn, [(w_gates, 0)], ep_sig, [(3 * d, F32, tn_g, None)], tn=tn_g, name="proj_gates")

    conv_p, hist_p = _short_conv(u, cb, jnp.zeros((bp, 2, cw), F32), conv_w[0], 0, bp, t_p)
    conv_s, hist_s = _short_conv(u, cb, state_conv[0], conv_w[0], n_p, bs, t_s)
    conv_pre = jnp.concatenate([conv_p, conv_s], axis=0)

    ki_b = ki_f.astype(BF16)
    l_p = t_p
    lpad_p = _pad_len(l_p, LANES)
    kit_p = jnp.pad(ki_b[:n_p].T, ((0, 0), (0, lpad_p - l_p)))[None]
    bias_p = _select(qi_r, wi, kit_p, row_off=0, nb=1, t_len=t_p, l_valid=l_p, causal=True,
                     ktop=min(TOPK_MAX, l_p // 4), idx_dim=idx_dim)
    kp = jnp.pad(k_b[:n_p], ((0, lpad_p - l_p), (0, 0)))[None]
    vp = jnp.pad(v_b[:n_p], ((0, lpad_p - l_p), (0, 0)))[None]
    attn_p = _attention(q_r, kp, vp, bias_p, row_off=0, nb=1, t_len=t_p, causal=True,
                        n_heads=n_heads, n_kv=n_kv, hd=hd)

    l_s = past + t_s
    lpad_s = _pad_len(l_s, 512)
    pad_s = lpad_s - l_s
    ki_all = jnp.concatenate([cache_idx_k[0].astype(BF16), ki_b[n_p:].reshape(bs, t_s, idx_dim)], axis=1)
    kit_s = jnp.pad(jnp.swapaxes(ki_all, 1, 2), ((0, 0), (0, 0), (0, pad_s)))
    bias_s = _select(qi_r, wi, kit_s, row_off=n_p, nb=bs, t_len=t_s, l_valid=l_s, causal=False,
                     ktop=min(TOPK_MAX, l_s // 4), idx_dim=idx_dim)
    ks = jnp.concatenate([cache_attn_k[0].reshape(bs, past, attn_kv).astype(BF16),
                          k_b[n_p:].reshape(bs, t_s, attn_kv)], axis=1)
    vs = jnp.concatenate([cache_attn_v[0].reshape(bs, past, attn_kv).astype(BF16),
                          v_b[n_p:].reshape(bs, t_s, attn_kv)], axis=1)
    ks = jnp.pad(ks, ((0, 0), (0, pad_s), (0, 0)))
    vs = jnp.pad(vs, ((0, 0), (0, pad_s), (0, 0)))
    attn_s = _attention(q_r, ks, vs, bias_s, row_off=n_p, nb=bs, t_len=t_s, causal=False,
                        n_heads=n_heads, n_kv=n_kv, hd=hd)
    attn = jnp.concatenate([attn_p, attn_s], axis=0)

    memn = _rmsnorm(mem_prompt.reshape(bp * n_mem, d), g_mem[0], BF16)
    tn_kv = _pick(2 * mem_q, 1024)

    def ep_f32(accs, extras, outs):
        outs[0][...] = accs[0]

    (mem_kv,) = _mm(memn, [(w_mem_kv[0].astype(BF16), 0)], ep_f32, [(2 * mem_q, F32, tn_kv, None)],
                    tn=tn_kv, name="proj_mem_kv")
    mk_p = mem_kv[:, :mem_q].reshape(bp, n_mem, mem_q)
    mv_p = mem_kv[:, mem_q:].reshape(bp, n_mem, mem_q)
    memo_p = _mem_attention(mq, mk_p, mv_p, row_off=0, nb=bp, t_len=t_p, n_heads=mem_heads, hd=mem_hd)
    memo_s = _mem_attention(mq, cache_mem_k[0].reshape(bs, n_mem, mem_q), cache_mem_v[0].reshape(bs, n_mem, mem_q),
                            row_off=n_p, nb=bs, t_len=t_s, n_heads=mem_heads, hd=mem_hd)
    memo = jnp.concatenate([memo_p, memo_s], axis=0)

    ngb = d // tn_g

    def gate_extra(which):
        return (sg, (tm, tn_g), functools.partial(lambda i, j, o: (i, o + j), o=which * ngb))

    def ep_gate0(accs, extras, outs):
        outs[0][...] = extras[0][...] * accs[0]

    def ep_gate(accs, extras, outs):
        outs[0][...] = (extras[1][...] + extras[0][...] * accs[0]).astype(outs[0].dtype)

    prev = lambda a: (a, (tm, tn_g), lambda i, j: (i, j))
    (m1,) = _mm(conv_pre, [(w_conv_proj[0].astype(BF16), 0)], ep_gate0, [(d, F32, tn_g, None)],
                extras=[gate_extra(0)], tn=tn_g, name="merge_conv")
    (m2,) = _mm(attn, [(w_attn_proj[0].astype(BF16), 0)], ep_gate, [(d, F32, tn_g, None)],
                extras=[gate_extra(1), prev(m1)], tn=tn_g, name="merge_attn")
    (m3,) = _mm(memo, [(w_mem_proj[0].astype(BF16), 0)], ep_gate, [(d, BF16, tn_g, None)],
                extras=[gate_extra(2), prev(m2)], tn=tn_g, name="merge_mem")

    def ep_res(accs, extras, outs):
        outs[0][...] = extras[0][...] + accs[0]

    (x1,) = _mm(m3, [(w_merge_out[0].astype(BF16), 0)], ep_res, [(d, F32, tn_g, None)],
                extras=[prev(x)], tn=tn_g, name="merge_out")

    xn2, ridx, rp = _router(x1, g_moe[0], w_router[0], b_router[0])
    ne = w_router.shape[2]
    ff = w_down.shape[2]
    tme = MOE_TILE if n * TOP_K >= 16 * MOE_TILE else 128
    nk = n * TOP_K
    e_flat = ridx[:, :TOP_K].reshape(nk)
    order = jnp.argsort(e_flat, stable=True).astype(I32)
    counts = jnp.sum((e_flat[:, None] == jnp.arange(ne, dtype=I32)[None, :]).astype(I32), axis=0)
    tiles_per = (counts + tme - 1) // tme
    tile_end = jnp.cumsum(tiles_per)
    n_used = tile_end[-1].astype(I32)
    gstart = (tile_end - tiles_per) * tme
    cstart = jnp.cumsum(counts) - counts
    mt = nk // tme + ne
    tile_ids = jnp.arange(mt, dtype=I32)
    tile_expert = jnp.minimum(jnp.sum((tile_ids[:, None] >= tile_end[None, :]).astype(I32), axis=1), ne - 1)
    last_e = tile_expert[jnp.maximum(n_used - 1, 0)]
    tile_expert = jnp.where(tile_ids < n_used, tile_expert, last_e).astype(I32)
    slot_ids = jnp.arange(mt * tme, dtype=I32)
    s_e = tile_expert[slot_ids // tme]
    s_r = slot_ids - gstart[s_e]
    s_valid = jnp.logical_and(s_r < counts[s_e], slot_ids < n_used * tme)
    s_pos = jnp.clip(cstart[s_e] + s_r, 0, nk - 1)
    slot_token = jnp.where(s_valid, order[s_pos] // TOP_K, 0).astype(I32).reshape(mt, 1, tme)
    sorted_e = e_flat[order]
    slot_of_sorted = gstart[sorted_e] + (jnp.arange(nk, dtype=I32) - cstart[sorted_e])
    pos_slot = jnp.zeros((nk,), I32).at[order].set(slot_of_sorted.astype(I32), unique_indices=True)

    y_slots = _moe_experts(xn2, tile_expert, n_used.reshape(1), slot_token, w_up_gate[0], b_up_gate[0],
                           w_down[0], b_down[0], tm=tme, fc=_pick(ff, 256))
    tmc = _pick(n, 256)
    y = _combine(x1, rp, pos_slot, y_slots, g_final, tm=tmc)

    y_prompt = y[:n_p].reshape(bp, t_p, d)
    y_sample = y[n_p:].reshape(bs, t_s, d)
    new_k_p = k_f[:n_p].reshape(1, bp, t_p, n_kv, hd)
    new_v_p = v_f[:n_p].reshape(1, bp, t_p, n_kv, hd)
    new_ki_p = ki_f[:n_p].reshape(1, bp, t_p, idx_dim)
    new_k_s = k_f[n_p:].reshape(1, bs, t_s, n_kv, hd)
    new_v_s = v_f[n_p:].reshape(1, bs, t_s, n_kv, hd)
    new_ki_s = ki_f[n_p:].reshape(1, bs, t_s, idx_dim)
    return (y_prompt, y_sample, new_k_p, new_v_p, new_ki_p, hist_p[None],
            mk_p.reshape(1, bp, n_mem, mem_heads, mem_hd), mv_p.reshape(1, bp, n_mem, mem_heads, mem_hd),
            new_k_s, new_v_s, new_ki_s, hist_s[None])
```

```python
import functools

import jax
import jax.numpy as jnp
from jax import lax
from jax.experimental import pallas as pl
from jax.experimental.pallas import tpu as pltpu

F32 = jnp.float32
BF16 = jnp.bfloat16
I32 = jnp.int32

CHUNK = 64
IDX_HEADS = 16
TOPK_MAX = 256
TOP_K = 4
SWIGLU_LIMIT = 7.0
SWIGLU_ALPHA = 1.702
ROPE_THETA = 10000.0
EPS = 1e-6

LANES = 128
VMEM_LIMIT = 56 * 1024 * 1024
LOG2E = 1.4426950408889634
ATT_ROWS = 32
COUNT_ROWS = 64
MOE_TILE = 1024
MASK_NEG = -1e30
INT_MIN = -2147483648
NEGINF_KEY = -2139095041


def _cparams(sem):
    return pltpu.CompilerParams(dimension_semantics=sem, vmem_limit_bytes=VMEM_LIMIT)


def _pick(n, pref):
    if n <= pref:
        return n
    t = pref
    while n % t:
        t //= 2
    return t


def _rms_kernel(x_ref, g_ref, o_ref):
    x = x_ref[...]
    ms = jnp.mean(x * x, axis=-1, keepdims=True)
    o_ref[...] = (x * lax.rsqrt(ms + EPS) * g_ref[...]).astype(o_ref.dtype)


def _rmsnorm(x, g, out_dtype):
    n, d = x.shape
    tm = _pick(n, 512)
    return pl.pallas_call(
        _rms_kernel,
        grid=(n // tm,),
        in_specs=[pl.BlockSpec((tm, d), lambda i: (i, 0)),
                  pl.BlockSpec((1, d), lambda i: (0, 0))],
        out_specs=pl.BlockSpec((tm, d), lambda i: (i, 0)),
        out_shape=jax.ShapeDtypeStruct((n, d), out_dtype),
        compiler_params=_cparams(("parallel",)),
        name="rmsnorm",
    )(x, g.reshape(1, d))


def _mm_kernel(epilogue, n_w, n_extra, x_ref, *refs):
    w_refs = refs[:n_w]
    extras = refs[n_w:n_w + n_extra]
    outs = refs[n_w + n_extra:]
    x = x_ref[...]
    accs = [jnp.dot(x, w[...], preferred_element_type=F32) for w in w_refs]
    epilogue(accs, extras, outs)


def _mm(x, ws, epilogue, outs, extras=(), *, tn, tm=512, name="mm"):
    n, k = x.shape
    tm = _pick(n, tm)
    ncol = outs[0][0] // outs[0][2]
    in_specs = [pl.BlockSpec((tm, k), lambda i, j: (i, 0))]
    args = [x]
    for w, off in ws:
        in_specs.append(pl.BlockSpec((k, tn), functools.partial(lambda i, j, o: (0, o + j), o=off)))
        args.append(w)
    for a, bs, im in extras:
        in_specs.append(pl.BlockSpec(bs, im))
        args.append(a)
    out_specs, out_shape = [], []
    for width, dt, bw, im in outs:
        out_specs.append(pl.BlockSpec((tm, bw), im if im is not None else (lambda i, j: (i, j))))
        out_shape.append(jax.ShapeDtypeStruct((n, width), dt))
    res = pl.pallas_call(
        functools.partial(_mm_kernel, epilogue, len(ws), len(extras)),
        grid=(n // tm, ncol),
        in_specs=in_specs,
        out_specs=out_specs,
        out_shape=out_shape,
        compiler_params=_cparams(("parallel", "arbitrary")),
        name=name,
    )(*args)
    return res


def _rope_groups(acc, cos, sin, half):
    tn = acc.shape[1]
    lane = lax.broadcasted_iota(I32, (acc.shape[0], LANES), 1)
    pieces = []
    for c in range(tn // LANES):
        a = acc[:, c * LANES:(c + 1) * LANES]
        if 2 * half == LANES:
            partner = pltpu.roll(a, half, axis=1)
        else:
            first = (lane & (2 * half - 1)) < half
            partner = jnp.where(first, pltpu.roll(a, LANES - half, axis=1), pltpu.roll(a, half, axis=1))
        pieces.append(a * cos + partner * sin)
    return pieces


def _conv_kernel(u_ref, cb_ref, h_ref, w_ref, o_ref, nh_ref, carry_ref):
    t = pl.program_id(1)

    @pl.when(t == 0)
    def _():
        carry_ref[0:2, :] = h_ref[...]

    u = u_ref[...]
    tm = u.shape[0]
    h0 = carry_ref[0:1, :]
    h1 = carry_ref[1:2, :]
    row = lax.broadcasted_iota(I32, u.shape, 0)
    p1 = jnp.where(row == 0, h1, pltpu.roll(u, 1, axis=0))
    p2 = jnp.where(row == 0, h0, jnp.where(row == 1, h1, pltpu.roll(u, 2, axis=0)))
    w = w_ref[...]
    y = w[0:1, :] * p2 + w[1:2, :] * p1 + w[2:3, :] * u
    o_ref[...] = (cb_ref[...] * y).astype(o_ref.dtype)
    last = u[tm - 2:tm, :]
    carry_ref[0:2, :] = last
    nh_ref[...] = last


def _short_conv(u, cb, hist, conv_w, row_off, nb, t_len):
    c = u.shape[1]
    tm = _pick(t_len, 512)
    nt = t_len // tm
    off = row_off // tm
    rows = lambda b, t: (off + b * nt + t, 0)
    conv_pre, new_hist = pl.pallas_call(
        _conv_kernel,
        grid=(nb, nt),
        in_specs=[pl.BlockSpec((tm, c), rows),
                  pl.BlockSpec((tm, c), rows),
                  pl.BlockSpec((None, 2, c), lambda b, t: (b, 0, 0)),
                  pl.BlockSpec((3, c), lambda b, t: (0, 0))],
        out_specs=[pl.BlockSpec((tm, c), lambda b, t: (b * nt + t, 0)),
                   pl.BlockSpec((None, 2, c), lambda b, t: (b, 0, 0))],
        out_shape=[jax.ShapeDtypeStruct((nb * t_len, c), BF16),
                   jax.ShapeDtypeStruct((nb, 2, c), F32)],
        scratch_shapes=[pltpu.VMEM((8, c), F32)],
        compiler_params=_cparams(("arbitrary", "arbitrary")),
        name="short_conv",
    )(u, cb, hist, conv_w)
    return conv_pre, new_hist


def _sel_kernel(ki_ref, qit_ref, wit_ref, bias_ref, key_ref, pcut_ref, *,
                tq, ch, cc, lpad, l_valid, causal, ktop, idx_dim):
    i = pl.program_id(1)
    nch_all = lpad // ch
    if causal:
        nch = jnp.minimum(((i + 1) * tq + ch - 1) // ch, nch_all)
    else:
        nch = nch_all
    scale = idx_dim ** -0.5
    qit = qit_ref[...]
    w = wit_ref[...]
    qh = [qit[h * idx_dim:(h + 1) * idx_dim, :] for h in range(IDX_HEADS)]
    if causal:
        q_pos = i * tq + lax.broadcasted_iota(I32, (1, tq), 1)
        k_lim = jnp.minimum((q_pos // CHUNK + 1) * CHUNK, l_valid)
    else:
        k_lim = l_valid
    row_iota = lax.broadcasted_iota(I32, (ch, tq), 0)
    row_iota_cc = lax.broadcasted_iota(I32, (cc, tq), 0)

    def score_chunk(c, carry):
        k0 = pl.multiple_of(c * ch, ch)
        kc = ki_ref[pl.ds(k0, ch), :]
        acc = jnp.zeros((ch, tq), F32)
        for h in range(IDX_HEADS):
            d = jnp.dot(kc, qh[h], preferred_element_type=F32)
            acc = acc + jnp.maximum(d, 0.0) * w[h:h + 1, :]
        s = acc * scale
        bits = lax.bitcast_convert_type(s, I32)
        key = jnp.where(bits < 0, bits ^ 0x7FFFFFFF, bits)
        key_ref[pl.ds(k0, ch), :] = jnp.where((k0 + row_iota) < k_lim, key, NEGINF_KEY)
        return carry

    lax.fori_loop(0, nch, score_chunk, 0)

    ncc = (nch * ch + cc - 1) // cc

    def neg_chunk(c, carry):
        k0 = pl.multiple_of(c * ch, ch)
        key_ref[pl.ds(k0, ch), :] = jnp.full((ch, tq), NEGINF_KEY, I32)
        return carry

    lax.fori_loop(nch, ncc * (cc // ch), neg_chunk, 0)

    def count(pred):
        def body(c, cnt):
            k0 = pl.multiple_of(c * cc, cc)
            blk = key_ref[pl.ds(k0, cc), :]
            ones = jnp.where(pred(blk, k0), 1.0, 0.0)
            return cnt + jnp.sum(ones.reshape(cc // COUNT_ROWS, COUNT_ROWS, tq), axis=0)
        cnt = lax.fori_loop(0, ncc, body, jnp.zeros((COUNT_ROWS, tq), F32))
        return jnp.sum(cnt, axis=0, keepdims=True)

    kf = float(ktop)
    n_adm = count(lambda blk, k0: blk > NEGINF_KEY)
    all_adm = n_adm <= kf
    zero = jnp.zeros((1, tq), I32)
    c0 = count(lambda blk, k0: blk >= zero)
    cand0 = jnp.where(c0 >= kf, 0, INT_MIN).astype(I32)
    cnt0 = jnp.where(c0 >= kf, c0, kf + 1.0)

    def settled(cnt):
        return jnp.min(jnp.where(jnp.logical_or(all_adm, cnt == kf), 1.0, 0.0)) > 0.0

    def bit_cond(st):
        bit, _, cnt = st
        return jnp.logical_and(bit >= 0, jnp.logical_not(settled(cnt)))

    def bit_body(st):
        bit, cand, cnt = st
        trial = cand | lax.shift_left(jnp.int32(1), bit)
        c = count(lambda blk, k0: blk >= trial)
        take = c >= kf
        return bit - 1, jnp.where(take, trial, cand), jnp.where(take, c, cnt)

    _, cand, cnt = lax.while_loop(bit_cond, bit_body, (jnp.int32(30), cand0, cnt0))
    thr = jnp.where(all_adm, NEGINF_KEY + 1, cand)
    pcut_ref[...] = jnp.full((1, tq), lpad, I32)

    @pl.when(jnp.logical_not(settled(cnt)))
    def _():
        tie = jnp.logical_not(jnp.logical_or(all_adm, cnt == kf))
        need = kf - count(lambda blk, k0: blk > thr)
        nbits = max(1, (lpad - 1).bit_length())

        def pbody(it, q):
            trial = q | lax.shift_left(jnp.int32(1), nbits - 1 - it)
            c = count(lambda blk, k0: jnp.logical_and(blk == thr, (k0 + row_iota_cc) < trial))
            return jnp.where(c < need, trial, q)

        q = lax.fori_loop(0, nbits, pbody, jnp.zeros((1, tq), I32))
        pcut_ref[...] = jnp.where(tie, q + 1, lpad)

    pcut = pcut_ref[...]

    def write_chunk(c, carry):
        k0 = pl.multiple_of(c * ch, ch)
        blk = key_ref[pl.ds(k0, ch), :]
        sel = jnp.logical_or(blk > thr, jnp.logical_and(blk == thr, (k0 + row_iota) < pcut))
        bias_ref[pl.ds(k0, ch), :] = jnp.where(sel, 0.0, MASK_NEG).astype(bias_ref.dtype)
        return carry

    lax.fori_loop(0, nch, write_chunk, 0)

    def fill_chunk(c, carry):
        k0 = pl.multiple_of(c * ch, ch)
        bias_ref[pl.ds(k0, ch), :] = jnp.full((ch, tq), MASK_NEG, bias_ref.dtype)
        return carry

    lax.fori_loop(nch, nch_all, fill_chunk, 0)


def _select(ki, qit, wit, *, l_valid, causal, ktop, idx_dim):
    nb, lpad, _ = ki.shape
    t_len = qit.shape[2]
    tq = _pick(t_len, 256)
    ch = _pick(lpad, 512)
    cc = next(c for c in (4 * ch, 3 * ch, 2 * ch, ch) if lpad % c == 0)
    assert ch >= ktop and lpad % ch == 0 and ch % COUNT_ROWS == 0
    kern = functools.partial(_sel_kernel, tq=tq, ch=ch, cc=cc, lpad=lpad, l_valid=l_valid,
                             causal=causal, ktop=ktop, idx_dim=idx_dim)
    return pl.pallas_call(
        kern,
        grid=(nb, t_len // tq),
        in_specs=[pl.BlockSpec((None, lpad, idx_dim), lambda b, i: (b, 0, 0)),
                  pl.BlockSpec((None, qit.shape[1], tq), lambda b, i: (b, 0, i)),
                  pl.BlockSpec((None, wit.shape[1], tq), lambda b, i: (b, 0, i))],
        out_specs=pl.BlockSpec((None, lpad, tq), lambda b, i: (b, 0, i)),
        out_shape=jax.ShapeDtypeStruct((nb, lpad, t_len), BF16),
        scratch_shapes=[pltpu.VMEM((lpad, tq), I32), pltpu.VMEM((1, tq), I32)],
        compiler_params=_cparams(("parallel", "arbitrary")),
        name="index_select",
    )(ki, qit, wit)


def _att_kernel(q_ref, k_ref, vt_ref, b_ref, o_ref, m_ref, l_ref, acc_ref, s_ref, p_ref, *,
                tq, tk, causal, n_kv, hd, rep):
    i = pl.program_id(1)
    j = pl.program_id(2)
    nj = pl.num_programs(2)

    @pl.when(j == 0)
    def _():
        m_ref[...] = jnp.full(m_ref.shape, MASK_NEG, F32)
        l_ref[...] = jnp.zeros(l_ref.shape, F32)
        acc_ref[...] = jnp.zeros(acc_ref.shape, F32)

    def compute():
        bias = b_ref[...].astype(F32)
        bias = jnp.concatenate([bias] * rep, axis=1)
        r = rep * tq
        nblk = tk // ATT_ROWS
        for g in range(n_kv):
            k = k_ref[:, g * hd:(g + 1) * hd]
            s_ref[...] = jnp.dot(k, q_ref[g], preferred_element_type=F32) + bias

            def max_body(b, m8):
                blk = s_ref[pl.ds(pl.multiple_of(b * ATT_ROWS, ATT_ROWS), ATT_ROWS), :]
                return jnp.maximum(m8, jnp.max(blk.reshape(ATT_ROWS // 8, 8, r), axis=0))

            m8 = lax.fori_loop(0, nblk, max_body, jnp.full((8, r), MASK_NEG, F32), unroll=True)
            m_prev = m_ref[g]
            m_new = jnp.maximum(m_prev, jnp.max(m8, axis=0, keepdims=True))
            alpha = jnp.exp2(m_prev - m_new)

            def exp_body(b, l8):
                r0 = pl.multiple_of(b * ATT_ROWS, ATT_ROWS)
                p = jnp.exp2(s_ref[pl.ds(r0, ATT_ROWS), :] - m_new)
                p_ref[pl.ds(r0, ATT_ROWS), :] = p.astype(BF16)
                return l8 + jnp.sum(p.reshape(ATT_ROWS // 8, 8, r), axis=0)

            l8 = lax.fori_loop(0, nblk, exp_body, jnp.zeros((8, r), F32), unroll=True)
            l_ref[g] = alpha * l_ref[g] + jnp.sum(l8, axis=0, keepdims=True)
            pv = jnp.dot(vt_ref[g * hd:(g + 1) * hd, :], p_ref[...], preferred_element_type=F32)
            acc_ref[g] = alpha * acc_ref[g] + pv
            m_ref[g] = m_new

    if causal:
        pl.when(j * tk < (i + 1) * tq)(compute)
    else:
        compute()

    @pl.when(j == nj - 1)
    def _():
        for g in range(n_kv):
            o_ref[g] = (acc_ref[g] / l_ref[g]).astype(o_ref.dtype)


def _attention(qt, k, vt, bias_t, *, tq, causal):
    nb, nt, n_kv, hd, r = qt.shape
    rep = r // tq
    lpad = k.shape[1]
    tk = _pick(lpad, 512)
    nk = lpad // tk
    if causal:
        last = lambda i: ((i + 1) * tq - 1) // tk
        kmap = lambda b, i, j: (b, jnp.minimum(j, last(i)), 0)
        vmap = lambda b, i, j: (b, 0, jnp.minimum(j, last(i)))
        bmap = lambda b, i, j: (b, jnp.minimum(j, last(i)), i)
    else:
        kmap = lambda b, i, j: (b, j, 0)
        vmap = lambda b, i, j: (b, 0, j)
        bmap = lambda b, i, j: (b, j, i)
    kern = functools.partial(_att_kernel, tq=tq, tk=tk, causal=causal, n_kv=n_kv, hd=hd, rep=rep)
    return pl.pallas_call(
        kern,
        grid=(nb, nt, nk),
        in_specs=[pl.BlockSpec((None, None, n_kv, hd, r), lambda b, i, j: (b, i, 0, 0, 0)),
                  pl.BlockSpec((None, tk, n_kv * hd), kmap),
                  pl.BlockSpec((None, n_kv * hd, tk), vmap),
                  pl.BlockSpec((None, tk, tq), bmap)],
        out_specs=pl.BlockSpec((None, None, n_kv, hd, r), lambda b, i, j: (b, i, 0, 0, 0)),
        out_shape=jax.ShapeDtypeStruct(qt.shape, BF16),
        scratch_shapes=[pltpu.VMEM((n_kv, 1, r), F32),
                        pltpu.VMEM((n_kv, 1, r), F32),
                        pltpu.VMEM((n_kv, hd, r), F32),
                        pltpu.VMEM((tk, r), F32),
                        pltpu.VMEM((tk, r), BF16)],
        compiler_params=_cparams(("parallel", "parallel", "arbitrary")),
        name="sparse_attention",
    )(qt, k, vt, bias_t)


def _mematt_kernel(q_ref, k_ref, v_ref, o_ref, *, n_heads, hd):
    scale = hd ** -0.5
    for h in range(n_heads):
        q = q_ref[:, h * hd:(h + 1) * hd]
        k = k_ref[:, h * hd:(h + 1) * hd].astype(BF16)
        v = v_ref[:, h * hd:(h + 1) * hd].astype(BF16)
        s = lax.dot_general(q, k, (((1,), (1,)), ((), ())), preferred_element_type=F32) * scale
        m = jnp.max(s, axis=-1, keepdims=True)
        p = jnp.exp(s - m)
        l = jnp.sum(p, axis=-1, keepdims=True)
        o = jnp.dot(p.astype(BF16), v, preferred_element_type=F32) / l
        o_ref[:, h * hd:(h + 1) * hd] = o.astype(o_ref.dtype)


def _mem_attention(mq, mk, mv, *, row_off, nb, t_len, n_heads, hd):
    tq = _pick(t_len, 512)
    nt = t_len // tq
    off = row_off // tq
    n_mem = mk.shape[1]
    w = n_heads * hd
    return pl.pallas_call(
        functools.partial(_mematt_kernel, n_heads=n_heads, hd=hd),
        grid=(nb, nt),
        in_specs=[pl.BlockSpec((tq, w), lambda b, i: (off + b * nt + i, 0)),
                  pl.BlockSpec((None, n_mem, w), lambda b, i: (b, 0, 0)),
                  pl.BlockSpec((None, n_mem, w), lambda b, i: (b, 0, 0))],
        out_specs=pl.BlockSpec((tq, w), lambda b, i: (b * nt + i, 0)),
        out_shape=jax.ShapeDtypeStruct((nb * t_len, w), BF16),
        compiler_params=_cparams(("parallel", "parallel")),
        name="memory_attention",
    )(mq, mk, mv)


def _router_kernel(x_ref, g_ref, w_ref, b_ref, xn_ref, idx_ref, p_ref):
    x = x_ref[...]
    ms = jnp.mean(x * x, axis=-1, keepdims=True)
    xn = x * lax.rsqrt(ms + EPS) * g_ref[...]
    xn_ref[...] = xn
    logits = jnp.dot(xn.astype(BF16), w_ref[...], preferred_element_type=F32) + b_ref[...]
    lane = lax.broadcasted_iota(I32, logits.shape, 1)
    lane_f = lane.astype(F32)
    vals, idxs = [], []
    l = logits
    for _ in range(TOP_K):
        m = jnp.max(l, axis=-1, keepdims=True)
        ix = jnp.min(jnp.where(l == m, lane_f, float(LANES)), axis=-1, keepdims=True)
        vals.append(m)
        idxs.append(ix)
        l = jnp.where(lane_f == ix, -jnp.inf, l)
    es = [jnp.exp(v - vals[0]) for v in vals]
    den = es[0]
    for e in es[1:]:
        den = den + e
    idx_out = jnp.zeros(logits.shape, F32)
    p_out = jnp.zeros(logits.shape, F32)
    for r in range(TOP_K):
        idx_out = jnp.where(lane == r, idxs[r], idx_out)
        p_out = jnp.where(lane == r, es[r] / den, p_out)
    idx_ref[...] = idx_out.astype(I32)
    p_ref[...] = p_out


def _router(x1, g, w_router, b_router):
    n, d = x1.shape
    ne = w_router.shape[1]
    tm = _pick(n, 512)
    w_pad = jnp.zeros((d, LANES), BF16).at[:, :ne].set(w_router.astype(BF16))
    b_pad = jnp.full((1, LANES), MASK_NEG, F32).at[0, :ne].set(b_router)
    return pl.pallas_call(
        _router_kernel,
        grid=(n // tm,),
        in_specs=[pl.BlockSpec((tm, d), lambda i: (i, 0)),
                  pl.BlockSpec((1, d), lambda i: (0, 0)),
                  pl.BlockSpec((d, LANES), lambda i: (0, 0)),
                  pl.BlockSpec((1, LANES), lambda i: (0, 0))],
        out_specs=[pl.BlockSpec((tm, d), lambda i: (i, 0)),
                   pl.BlockSpec((tm, LANES), lambda i: (i, 0)),
                   pl.BlockSpec((tm, LANES), lambda i: (i, 0))],
        out_shape=[jax.ShapeDtypeStruct((n, d), F32),
                   jax.ShapeDtypeStruct((n, LANES), I32),
                   jax.ShapeDtypeStruct((n, LANES), F32)],
        compiler_params=_cparams(("parallel",)),
        name="router",
    )(x1, g.reshape(1, d), w_pad, b_pad)


def _row_copy(src, dst, src_row, dst_row, sem):
    return pltpu.make_async_copy(src.at[pl.ds(src_row, 1)], dst.at[pl.ds(dst_row, 1)], sem)


def _moe_kernel(te_ref, tp_ref, tr_ref, nu_ref, ord_ref, x_hbm, wg_ref, wu_ref, bg_ref, bu_ref, wd_ref, bd_ref,
                y_hbm, xf_ref, xb_ref, acc_ref, gsem, ssem, *, tm, nf):
    t = pl.program_id(0)
    f = pl.program_id(1)
    shift = TOP_K.bit_length() - 1

    def scatter(tile, op):
        base = tp_ref[tile]

        def body(r, c):
            cp = _row_copy(acc_ref, y_hbm, r, ord_ref[base + r], ssem)
            cp.start() if op == "start" else cp.wait()
            return c

        lax.fori_loop(0, tr_ref[tile], body, 0)

    @pl.when(t < nu_ref[0])
    def _():
        @pl.when(f == 0)
        def _():
            base = tp_ref[t]

            def issue(r, c):
                _row_copy(x_hbm, xf_ref, lax.shift_right_logical(ord_ref[base + r], shift), r, gsem).start()
                return c

            lax.fori_loop(0, tm, issue, 0)

            def wait(r, c):
                _row_copy(x_hbm, xf_ref, lax.shift_right_logical(ord_ref[base + r], shift), r, gsem).wait()
                return c

            lax.fori_loop(0, tm, wait, 0)
            xb_ref[...] = xf_ref[...].astype(BF16)

        xb = xb_ref[...]
        hg = jnp.dot(xb, wg_ref[...], preferred_element_type=F32) + bg_ref[...]
        hu = jnp.dot(xb, wu_ref[...], preferred_element_type=F32) + bu_ref[...]
        gate = jnp.minimum(hg, SWIGLU_LIMIT)
        up = jnp.clip(hu, -SWIGLU_LIMIT, SWIGLU_LIMIT)
        act = (up + 1.0) * gate * jax.nn.sigmoid(SWIGLU_ALPHA * gate)
        contrib = jnp.dot(act.astype(BF16), wd_ref[...], preferred_element_type=F32)

        @pl.when(f == 0)
        def _():
            @pl.when(t > 0)
            def _():
                scatter(t - 1, "wait")

            acc_ref[...] = contrib + bd_ref[...]

        @pl.when(f > 0)
        def _():
            acc_ref[...] += contrib

        @pl.when(f == nf - 1)
        def _():
            scatter(t, "start")

            @pl.when(t == nu_ref[0] - 1)
            def _():
                scatter(t, "wait")


def _moe_experts(xn, tile_expert, tile_pos, tile_rows, n_used, order, w_up_gate, b_up_gate, w_down, b_down, *, tm, fc):
    n, d = xn.shape
    ne, _, ff2 = w_up_gate.shape
    ff = ff2 // 2
    nf = ff // fc
    mt = tile_expert.shape[0]

    def fe(t, f, nu):
        return jnp.where(t < nu[0], f, nf - 1)

    grid_spec = pltpu.PrefetchScalarGridSpec(
        num_scalar_prefetch=5,
        grid=(mt, nf),
        in_specs=[
            pl.BlockSpec(memory_space=pl.ANY),
            pl.BlockSpec((None, d, fc), lambda t, f, te, tp, tr, nu, o: (te[t], 0, fe(t, f, nu))),
            pl.BlockSpec((None, d, fc), lambda t, f, te, tp, tr, nu, o: (te[t], 0, nf + fe(t, f, nu))),
            pl.BlockSpec((None, 1, fc), lambda t, f, te, tp, tr, nu, o: (te[t], 0, fe(t, f, nu))),
            pl.BlockSpec((None, 1, fc), lambda t, f, te, tp, tr, nu, o: (te[t], 0, nf + fe(t, f, nu))),
            pl.BlockSpec((None, fc, d), lambda t, f, te, tp, tr, nu, o: (te[t], fe(t, f, nu), 0)),
            pl.BlockSpec((None, 1, d), lambda t, f, te, tp, tr, nu, o: (te[t], 0, 0)),
        ],
        out_specs=pl.BlockSpec(memory_space=pl.ANY),
        scratch_shapes=[pltpu.VMEM((tm, d), F32), pltpu.VMEM((tm, d), BF16), pltpu.VMEM((tm, d), F32),
                        pltpu.SemaphoreType.DMA(()), pltpu.SemaphoreType.DMA(())],
    )
    return pl.pallas_call(
        functools.partial(_moe_kernel, tm=tm, nf=nf),
        grid_spec=grid_spec,
        out_shape=jax.ShapeDtypeStruct((n * TOP_K, d), F32),
        compiler_params=_cparams(("arbitrary", "arbitrary")),
        name="moe_experts",
    )(tile_expert, tile_pos, tile_rows, n_used, order, xn, w_up_gate, w_up_gate,
      b_up_gate.reshape(ne, 1, ff2), b_up_gate.reshape(ne, 1, ff2), w_down, b_down.reshape(ne, 1, d))


def _combine_kernel(x_ref, p_ref, g_ref, y_ref, o_ref):
    d = x_ref.shape[1]
    p = p_ref[...]
    x = x_ref[...]
    for k in range(TOP_K):
        x = x + p[:, k:k + 1] * y_ref[:, k * d:(k + 1) * d]
    ms = jnp.mean(x * x, axis=-1, keepdims=True)
    o_ref[...] = x * lax.rsqrt(ms + EPS) * g_ref[...]


def _combine(x1, probs, y, g_final):
    n, d = x1.shape
    tm = _pick(n, 256)
    return pl.pallas_call(
        _combine_kernel,
        grid=(n // tm,),
        in_specs=[pl.BlockSpec((tm, d), lambda i: (i, 0)),
                  pl.BlockSpec((tm, LANES), lambda i: (i, 0)),
                  pl.BlockSpec((1, d), lambda i: (0, 0)),
                  pl.BlockSpec((tm, TOP_K * d), lambda i: (i, 0))],
        out_specs=pl.BlockSpec((tm, d), lambda i: (i, 0)),
        out_shape=jax.ShapeDtypeStruct((n, d), F32),
        compiler_params=_cparams(("parallel",)),
        name="moe_combine",
    )(x1, probs, g_final.reshape(1, d), y)


def _rope_tables(pos, dim):
    half = dim // 2
    inv_freq = ROPE_THETA ** (-jnp.arange(half, dtype=F32) / half)
    ang = pos.astype(F32)[:, None] * inv_freq[None, :]
    cos, sin = jnp.cos(ang), jnp.sin(ang)
    reps = LANES // dim
    cos_t = jnp.tile(jnp.concatenate([cos, cos], axis=-1), (1, reps))
    sin_t = jnp.tile(jnp.concatenate([-sin, sin], axis=-1), (1, reps))
    return cos_t, sin_t


def _pad_len(l, mult):
    return (l + mult - 1) // mult * mult


def kernel(x_prompt, x_sample, cache_attn_k, cache_attn_v, cache_idx_k, state_conv, cache_mem_k, cache_mem_v, mem_prompt, g_mix, w_comb, conv_w, w_conv_proj, w_attn_proj, g_mem, w_mem_kv, w_mem_proj, w_merge_out, g_moe, w_router, b_router, w_up_gate, b_up_gate, w_down, b_down, g_final):
    bp, t_p, d = x_prompt.shape
    bs, t_s, _ = x_sample.shape
    assert bp == 1 and g_mix.shape[0] == 1
    past = cache_attn_k.shape[2]
    n_kv, hd = cache_attn_k.shape[3], cache_attn_k.shape[4]
    idx_dim = cache_idx_k.shape[3]
    cw = conv_w.shape[2]
    attn_q = w_attn_proj.shape[1]
    n_heads = attn_q // hd
    attn_kv = n_kv * hd
    idx_q = IDX_HEADS * idx_dim
    n_mem, mem_heads, mem_hd = cache_mem_k.shape[2], cache_mem_k.shape[3], cache_mem_k.shape[4]
    mem_q = mem_heads * mem_hd
    n_p, n_s = bp * t_p, bs * t_s
    n = n_p + n_s

    o_cin, o_cb, o_cc = 0, cw, 2 * cw
    o_q = 3 * cw
    o_k = o_q + attn_q
    o_v = o_k + attn_kv
    o_qi = o_v + attn_kv
    o_ki = o_qi + idx_q
    o_wi = o_ki + idx_dim
    o_mq = o_wi + IDX_HEADS
    o_g = o_mq + mem_q
    wc = w_comb[0]
    w_main = wc[:, :o_ki].astype(BF16)
    w_kw = jnp.zeros((d, LANES), BF16).at[:, :idx_dim + IDX_HEADS].set(wc[:, o_ki:o_mq].astype(BF16))
    w_mq = wc[:, o_mq:o_g].astype(BF16)
    w_gates = wc[:, o_g:].astype(BF16)

    x = jnp.concatenate([x_prompt.reshape(n_p, d), x_sample.reshape(n_s, d)], axis=0)
    pos = jnp.concatenate([jnp.arange(t_p, dtype=I32),
                           jnp.tile(past + jnp.arange(t_s, dtype=I32), bs)])
    cos_h, sin_h = _rope_tables(pos, hd)
    cos_i, sin_i = _rope_tables(pos, idx_dim)
    tm = _pick(n, 512)
    rowtab = lambda a: (a, (tm, LANES), lambda i, j: (i, 0))

    xn = _rmsnorm(x, g_mix[0], BF16)

    tn_c = _pick(cw, 512)
    nbc = cw // tn_c

    def ep_conv(accs, extras, outs):
        outs[0][...] = accs[2] * accs[0]
        outs[1][...] = accs[1]

    u, cb = _mm(xn, [(w_main, 0), (w_main, nbc), (w_main, 2 * nbc)], ep_conv,
                [(cw, F32, tn_c, None), (cw, F32, tn_c, None)], tn=tn_c, name="proj_conv")

    attn_scale = hd ** -0.5 * LOG2E
    tn_q = _pick(attn_q, 1024)

    def ep_q(accs, extras, outs):
        pieces = _rope_groups(accs[0], extras[0][...], extras[1][...], hd // 2)
        for c, p in enumerate(pieces):
            outs[0][:, c * LANES:(c + 1) * LANES] = (p * attn_scale).astype(BF16)

    (q_r,) = _mm(xn, [(w_main, o_q // tn_q)], ep_q, [(attn_q, BF16, tn_q, None)],
                 extras=[rowtab(cos_h), rowtab(sin_h)], tn=tn_q, name="proj_q")

    def ep_k(accs, extras, outs):
        pieces = _rope_groups(accs[0], extras[0][...], extras[1][...], hd // 2)
        for c, p in enumerate(pieces):
            outs[0][:, c * LANES:(c + 1) * LANES] = p
            outs[1][:, c * LANES:(c + 1) * LANES] = p.astype(BF16)

    k_f, k_b = _mm(xn, [(w_main, o_k // attn_kv)], ep_k,
                   [(attn_kv, F32, attn_kv, None), (attn_kv, BF16, attn_kv, None)],
                   extras=[rowtab(cos_h), rowtab(sin_h)], tn=attn_kv, name="proj_k")

    def ep_v(accs, extras, outs):
        outs[0][...] = accs[0]
        outs[1][...] = accs[0].astype(BF16)

    v_f, v_b = _mm(xn, [(w_main, o_v // attn_kv)], ep_v,
                   [(attn_kv, F32, attn_kv, None), (attn_kv, BF16, attn_kv, None)],
                   tn=attn_kv, name="proj_v")

    tn_i = _pick(idx_q, 1024)

    def ep_qi(accs, extras, outs):
        pieces = _rope_groups(accs[0], extras[0][...], extras[1][...], idx_dim // 2)
        for c, p in enumerate(pieces):
            outs[0][:, c * LANES:(c + 1) * LANES] = p.astype(BF16)

    (qi_r,) = _mm(xn, [(w_main, o_qi // tn_i)], ep_qi, [(idx_q, BF16, tn_i, None)],
                  extras=[rowtab(cos_i), rowtab(sin_i)], tn=tn_i, name="proj_qi")

    wi_scale = IDX_HEADS ** -0.5

    def ep_kw(accs, extras, outs):
        a = accs[0]
        (roped,) = _rope_groups(a, extras[0][...], extras[1][...], idx_dim // 2)
        lane = lax.broadcasted_iota(I32, a.shape, 1)
        outs[0][...] = jnp.where(lane < idx_dim, roped, a * wi_scale)

    (kw,) = _mm(xn, [(w_kw, 0)], ep_kw, [(LANES, F32, LANES, None)],
                extras=[rowtab(cos_i), rowtab(sin_i)], tn=LANES, name="proj_ki_wi")
    ki_f = kw[:, :idx_dim]
    wi = kw[:, idx_dim:idx_dim + IDX_HEADS]

    def ep_cast(accs, extras, outs):
        outs[0][...] = accs[0].astype(outs[0].dtype)

    tn_m = _pick(mem_q, 1024)
    (mq,) = _mm(xn, [(w_mq, 0)], ep_cast, [(mem_q, BF16, tn_m, None)], tn=tn_m, name="proj_mq")

    def ep_sig(accs, extras, outs):
        outs[0][...] = jax.nn.sigmoid(accs[0])

    tn_g = _pick(d, 1024)
    (sg,) = _mm(xn, [(w_gates, 0)], ep_sig, [(3 * d, F32, tn_g, None)], tn=tn_g, name="proj_gates")

    conv_p, hist_p = _short_conv(u, cb, jnp.zeros((bp, 2, cw), F32), conv_w[0], 0, bp, t_p)
    conv_s, hist_s = _short_conv(u, cb, state_conv[0], conv_w[0], n_p, bs, t_s)
    conv_pre = jnp.concatenate([conv_p, conv_s], axis=0)

    rep = n_heads // n_kv
    ki_b = ki_f.astype(BF16)

    def to_qt(q, nb, t_len, tq):
        q = q.reshape(nb, t_len // tq, tq, n_kv, rep, hd)
        return jnp.transpose(q, (0, 1, 3, 5, 4, 2)).reshape(nb, t_len // tq, n_kv, hd, rep * tq)

    def from_qt(o, nb, t_len, tq):
        o = o.reshape(nb, t_len // tq, n_kv, hd, rep, tq)
        return jnp.transpose(o, (0, 1, 5, 2, 4, 3)).reshape(nb * t_len, attn_q)

    l_p = t_p
    lpad_p = _pad_len(l_p, LANES)
    pad_p = lpad_p - l_p
    bias_p = _select(jnp.pad(ki_b[:n_p], ((0, pad_p), (0, 0)))[None], qi_r[:n_p].T[None], wi[:n_p].T[None],
                     l_valid=l_p, causal=True, ktop=min(TOPK_MAX, l_p // 4), idx_dim=idx_dim)
    tq_p = _pick(t_p, 256)
    kp = jnp.pad(k_b[:n_p], ((0, pad_p), (0, 0)))[None]
    vtp = jnp.pad(v_b[:n_p].T, ((0, 0), (0, pad_p)))[None]
    attn_p = from_qt(_attention(to_qt(q_r[:n_p], bp, t_p, tq_p), kp, vtp, bias_p, tq=tq_p, causal=True),
                     bp, t_p, tq_p)

    l_s = past + t_s
    lpad_s = _pad_len(l_s, 512)
    pad_s = lpad_s - l_s
    ki_all = jnp.concatenate([cache_idx_k[0].astype(BF16), ki_b[n_p:].reshape(bs, t_s, idx_dim)], axis=1)
    bias_s = _select(jnp.pad(ki_all, ((0, 0), (0, pad_s), (0, 0))),
                     jnp.swapaxes(qi_r[n_p:].reshape(bs, t_s, idx_q), 1, 2),
                     jnp.swapaxes(wi[n_p:].reshape(bs, t_s, IDX_HEADS), 1, 2),
                     l_valid=l_s, causal=False, ktop=min(TOPK_MAX, l_s // 4), idx_dim=idx_dim)
    ks = jnp.concatenate([cache_attn_k[0].reshape(bs, past, attn_kv).astype(BF16),
                          k_b[n_p:].reshape(bs, t_s, attn_kv)], axis=1)
    vs = jnp.concatenate([cache_attn_v[0].reshape(bs, past, attn_kv).astype(BF16),
                          v_b[n_p:].reshape(bs, t_s, attn_kv)], axis=1)
    ks = jnp.pad(ks, ((0, 0), (0, pad_s), (0, 0)))
    vts = jnp.pad(jnp.swapaxes(vs, 1, 2), ((0, 0), (0, 0), (0, pad_s)))
    attn_s = from_qt(_attention(to_qt(q_r[n_p:], bs, t_s, t_s), ks, vts, bias_s, tq=t_s, causal=False),
                     bs, t_s, t_s)
    attn = jnp.concatenate([attn_p, attn_s], axis=0)

    memn = _rmsnorm(mem_prompt.reshape(bp * n_mem, d), g_mem[0], BF16)
    tn_kv = _pick(2 * mem_q, 1024)

    def ep_f32(accs, extras, outs):
        outs[0][...] = accs[0]

    (mem_kv,) = _mm(memn, [(w_mem_kv[0].astype(BF16), 0)], ep_f32, [(2 * mem_q, F32, tn_kv, None)],
                    tn=tn_kv, name="proj_mem_kv")
    mk_p = mem_kv[:, :mem_q].reshape(bp, n_mem, mem_q)
    mv_p = mem_kv[:, mem_q:].reshape(bp, n_mem, mem_q)
    memo_p = _mem_attention(mq, mk_p, mv_p, row_off=0, nb=bp, t_len=t_p, n_heads=mem_heads, hd=mem_hd)
    memo_s = _mem_attention(mq, cache_mem_k[0].reshape(bs, n_mem, mem_q), cache_mem_v[0].reshape(bs, n_mem, mem_q),
                            row_off=n_p, nb=bs, t_len=t_s, n_heads=mem_heads, hd=mem_hd)
    memo = jnp.concatenate([memo_p, memo_s], axis=0)

    ngb = d // tn_g

    def gate_extra(which):
        return (sg, (tm, tn_g), functools.partial(lambda i, j, o: (i, o + j), o=which * ngb))

    def ep_gate0(accs, extras, outs):
        outs[0][...] = extras[0][...] * accs[0]

    def ep_gate(accs, extras, outs):
        outs[0][...] = (extras[1][...] + extras[0][...] * accs[0]).astype(outs[0].dtype)

    prev = lambda a: (a, (tm, tn_g), lambda i, j: (i, j))
    (m1,) = _mm(conv_pre, [(w_conv_proj[0].astype(BF16), 0)], ep_gate0, [(d, F32, tn_g, None)],
                extras=[gate_extra(0)], tn=tn_g, name="merge_conv")
    (m2,) = _mm(attn, [(w_attn_proj[0].astype(BF16), 0)], ep_gate, [(d, F32, tn_g, None)],
                extras=[gate_extra(1), prev(m1)], tn=tn_g, name="merge_attn")
    (m3,) = _mm(memo, [(w_mem_proj[0].astype(BF16), 0)], ep_gate, [(d, BF16, tn_g, None)],
                extras=[gate_extra(2), prev(m2)], tn=tn_g, name="merge_mem")

    def ep_res(accs, extras, outs):
        outs[0][...] = extras[0][...] + accs[0]

    (x1,) = _mm(m3, [(w_merge_out[0].astype(BF16), 0)], ep_res, [(d, F32, tn_g, None)],
                extras=[prev(x)], tn=tn_g, name="merge_out")

    xn2, ridx, rp = _router(x1, g_moe[0], w_router[0], b_router[0])
    ne = w_router.shape[2]
    ff = w_down.shape[2]
    nk = n * TOP_K
    tme = MOE_TILE if nk >= 16 * MOE_TILE else 128
    e_flat = ridx[:, :TOP_K].reshape(nk)
    order = jnp.argsort(e_flat, stable=True).astype(I32)
    experts = jnp.arange(ne, dtype=I32)
    counts = jnp.sum((e_flat[:, None] == experts[None, :]).astype(I32), axis=0)
    tiles_per = (counts + tme - 1) // tme
    tile_end = jnp.cumsum(tiles_per)
    tile_first = tile_end - tiles_per
    cstart = jnp.cumsum(counts) - counts
    n_used = tile_end[-1].astype(I32)
    mt = nk // tme + ne
    tile_ids = jnp.arange(mt, dtype=I32)
    owner = jnp.logical_and(tile_ids[:, None] >= tile_first[None, :], tile_ids[:, None] < tile_end[None, :])
    pick = lambda v: jnp.sum(jnp.where(owner, v[None, :], 0), axis=1).astype(I32)
    k_in = tile_ids - pick(tile_first)
    tile_pos = pick(cstart) + k_in * tme
    tile_rows = jnp.clip(pick(counts) - k_in * tme, 0, tme)
    last_e = jnp.sum(jnp.where(tile_ids == n_used - 1, pick(experts), 0))
    tile_expert = jnp.where(tile_ids < n_used, pick(experts), last_e).astype(I32)
    order_pad = jnp.concatenate([order, jnp.zeros((tme,), I32)])
    y_assign = _moe_experts(xn2, tile_expert, tile_pos, tile_rows, n_used.reshape(1), order_pad,
                            w_up_gate[0].astype(BF16), b_up_gate[0], w_down[0].astype(BF16), b_down[0],
                            tm=tme, fc=_pick(ff, 512))
    y = _combine(x1, rp, y_assign.reshape(n, TOP_K * d), g_final)

    y_prompt = y[:n_p].reshape(bp, t_p, d)
    y_sample = y[n_p:].reshape(bs, t_s, d)
    new_k_p = k_f[:n_p].reshape(1, bp, t_p, n_kv, hd)
    new_v_p = v_f[:n_p].reshape(1, bp, t_p, n_kv, hd)
    new_ki_p = ki_f[:n_p].reshape(1, bp, t_p, idx_dim)
    new_k_s = k_f[n_p:].reshape(1, bs, t_s, n_kv, hd)
    new_v_s = v_f[n_p:].reshape(1, bs, t_s, n_kv, hd)
    new_ki_s = ki_f[n_p:].reshape(1, bs, t_s, idx_dim)
    return (y_prompt, y_sample, new_k_p, new_v_p, new_ki_p, hist_p[None],
            mk_p.reshape(1, bp, n_mem, mem_heads, mem_hd), mv_p.reshape(1, bp, n_mem, mem_heads, mem_hd),
            new_k_s, new_v_s, new_ki_s, hist_s[None])
```

```python
import functools

import jax
import jax.numpy as jnp
from jax import lax
from jax.experimental import pallas as pl
from jax.experimental.pallas import tpu as pltpu

F32 = jnp.float32
BF16 = jnp.bfloat16
I32 = jnp.int32

CHUNK = 64
IDX_HEADS = 16
TOPK_MAX = 256
TOP_K = 4
SWIGLU_LIMIT = 7.0
SWIGLU_ALPHA = 1.702
ROPE_THETA = 10000.0
EPS = 1e-6

LANES = 128
VMEM_LIMIT = 56 * 1024 * 1024
LOG2E = 1.4426950408889634
ATT_ROWS = 32
COUNT_ROWS = 64
MOE_TILE = 1024
MASK_NEG = -1e30
INT_MIN = -2147483648
NEGINF_KEY = -2139095041


def _cparams(sem):
    return pltpu.CompilerParams(dimension_semantics=sem, vmem_limit_bytes=VMEM_LIMIT)


def _pick(n, pref):
    if n <= pref:
        return n
    t = pref
    while n % t:
        t //= 2
    return t


def _rms_kernel(x_ref, g_ref, o_ref):
    x = x_ref[...]
    ms = jnp.mean(x * x, axis=-1, keepdims=True)
    o_ref[...] = (x * lax.rsqrt(ms + EPS) * g_ref[...]).astype(o_ref.dtype)


def _rmsnorm(x, g, out_dtype):
    n, d = x.shape
    tm = _pick(n, 512)
    return pl.pallas_call(
        _rms_kernel,
        grid=(n // tm,),
        in_specs=[pl.BlockSpec((tm, d), lambda i: (i, 0)),
                  pl.BlockSpec((1, d), lambda i: (0, 0))],
        out_specs=pl.BlockSpec((tm, d), lambda i: (i, 0)),
        out_shape=jax.ShapeDtypeStruct((n, d), out_dtype),
        compiler_params=_cparams(("parallel",)),
        name="rmsnorm",
    )(x, g.reshape(1, d))


def _mm_kernel(epilogue, n_w, n_extra, x_ref, *refs):
    w_refs = refs[:n_w]
    extras = refs[n_w:n_w + n_extra]
    outs = refs[n_w + n_extra:]
    x = x_ref[...]
    accs = [jnp.dot(x, w[...], preferred_element_type=F32) for w in w_refs]
    epilogue(accs, extras, outs)


def _mm(x, ws, epilogue, outs, extras=(), *, tn, tm=512, name="mm"):
    n, k = x.shape
    tm = _pick(n, tm)
    ncol = outs[0][0] // outs[0][2]
    in_specs = [pl.BlockSpec((tm, k), lambda i, j: (i, 0))]
    args = [x]
    for w, off in ws:
        in_specs.append(pl.BlockSpec((k, tn), functools.partial(lambda i, j, o: (0, o + j), o=off)))
        args.append(w)
    for a, bs, im in extras:
        in_specs.append(pl.BlockSpec(bs, im))
        args.append(a)
    out_specs, out_shape = [], []
    for o in outs:
        if len(o) == 5:
            shape, dt, bs, im, _ = o
            out_specs.append(pl.BlockSpec(bs, im))
            out_shape.append(jax.ShapeDtypeStruct(shape, dt))
            continue
        width, dt, bw, im = o
        out_specs.append(pl.BlockSpec((tm, bw), im if im is not None else (lambda i, j: (i, j))))
        out_shape.append(jax.ShapeDtypeStruct((n, width), dt))
    res = pl.pallas_call(
        functools.partial(_mm_kernel, epilogue, len(ws), len(extras)),
        grid=(n // tm, ncol),
        in_specs=in_specs,
        out_specs=out_specs,
        out_shape=out_shape,
        compiler_params=_cparams(("parallel", "arbitrary")),
        name=name,
    )(*args)
    return res


def _rope_groups(acc, cos, sin, half):
    tn = acc.shape[1]
    lane = lax.broadcasted_iota(I32, (acc.shape[0], LANES), 1)
    pieces = []
    for c in range(tn // LANES):
        a = acc[:, c * LANES:(c + 1) * LANES]
        if 2 * half == LANES:
            partner = pltpu.roll(a, half, axis=1)
        else:
            first = (lane & (2 * half - 1)) < half
            partner = jnp.where(first, pltpu.roll(a, LANES - half, axis=1), pltpu.roll(a, half, axis=1))
        pieces.append(a * cos + partner * sin)
    return pieces


def _conv_kernel(u_ref, cb_ref, h_ref, w_ref, o_ref, nh_ref, carry_ref):
    t = pl.program_id(1)

    @pl.when(t == 0)
    def _():
        carry_ref[0:2, :] = h_ref[...]

    u = u_ref[...]
    tm = u.shape[0]
    h0 = carry_ref[0:1, :]
    h1 = carry_ref[1:2, :]
    row = lax.broadcasted_iota(I32, u.shape, 0)
    p1 = jnp.where(row == 0, h1, pltpu.roll(u, 1, axis=0))
    p2 = jnp.where(row == 0, h0, jnp.where(row == 1, h1, pltpu.roll(u, 2, axis=0)))
    w = w_ref[...]
    y = w[0:1, :] * p2 + w[1:2, :] * p1 + w[2:3, :] * u
    o_ref[...] = (cb_ref[...] * y).astype(o_ref.dtype)
    last = u[tm - 2:tm, :]
    carry_ref[0:2, :] = last
    nh_ref[...] = last


def _short_conv(u, cb, hist, conv_w, row_off, nb, t_len):
    c = u.shape[1]
    tm = _pick(t_len, 512)
    nt = t_len // tm
    off = row_off // tm
    rows = lambda b, t: (off + b * nt + t, 0)
    conv_pre, new_hist = pl.pallas_call(
        _conv_kernel,
        grid=(nb, nt),
        in_specs=[pl.BlockSpec((tm, c), rows),
                  pl.BlockSpec((tm, c), rows),
                  pl.BlockSpec((None, 2, c), lambda b, t: (b, 0, 0)),
                  pl.BlockSpec((3, c), lambda b, t: (0, 0))],
        out_specs=[pl.BlockSpec((tm, c), lambda b, t: (b * nt + t, 0)),
                   pl.BlockSpec((None, 2, c), lambda b, t: (b, 0, 0))],
        out_shape=[jax.ShapeDtypeStruct((nb * t_len, c), BF16),
                   jax.ShapeDtypeStruct((nb, 2, c), F32)],
        scratch_shapes=[pltpu.VMEM((8, c), F32)],
        compiler_params=_cparams(("arbitrary", "arbitrary")),
        name="short_conv",
    )(u, cb, hist, conv_w)
    return conv_pre, new_hist


def _sel_kernel(ki_ref, qit_ref, wit_ref, bias_ref, key_ref, pcut_ref, *,
                tq, ch, cc, lpad, l_valid, causal, ktop, idx_dim):
    i = pl.program_id(1)
    nch_all = lpad // ch
    if causal:
        nch = jnp.minimum(((i + 1) * tq + ch - 1) // ch, nch_all)
    else:
        nch = nch_all
    scale = idx_dim ** -0.5
    qit = qit_ref[...]
    w = wit_ref[...]
    qh = [qit[h * idx_dim:(h + 1) * idx_dim, :] for h in range(IDX_HEADS)]
    if causal:
        q_pos = i * tq + lax.broadcasted_iota(I32, (1, tq), 1)
        k_lim = jnp.minimum((q_pos // CHUNK + 1) * CHUNK, l_valid)
    else:
        k_lim = l_valid
    row_iota = lax.broadcasted_iota(I32, (ch, tq), 0)
    row_iota_cc = lax.broadcasted_iota(I32, (cc, tq), 0)

    def score_chunk(c, carry):
        k0 = pl.multiple_of(c * ch, ch)
        kc = ki_ref[pl.ds(k0, ch), :]
        acc = jnp.zeros((ch, tq), F32)
        for h in range(IDX_HEADS):
            d = jnp.dot(kc, qh[h], preferred_element_type=F32)
            acc = acc + jnp.maximum(d, 0.0) * w[h:h + 1, :]
        key_ref[pl.ds(k0, ch), :] = jnp.where((k0 + row_iota) < k_lim, acc * scale, -jnp.inf)
        return carry

    lax.fori_loop(0, nch, score_chunk, 0)

    ncc = (nch * ch + cc - 1) // cc

    def neg_chunk(c, carry):
        k0 = pl.multiple_of(c * ch, ch)
        key_ref[pl.ds(k0, ch), :] = jnp.full((ch, tq), -jnp.inf, F32)
        return carry

    lax.fori_loop(nch, ncc * (cc // ch), neg_chunk, 0)

    def count(pred):
        def body(c, cnt):
            k0 = pl.multiple_of(c * cc, cc)
            blk = key_ref[pl.ds(k0, cc), :]
            ones = jnp.where(pred(blk, k0), 1.0, 0.0)
            return cnt + jnp.sum(ones.reshape(cc // COUNT_ROWS, COUNT_ROWS, tq), axis=0)
        cnt = lax.fori_loop(0, ncc, body, jnp.zeros((COUNT_ROWS, tq), F32))
        return jnp.sum(cnt, axis=0, keepdims=True)

    def as_score(key):
        return lax.bitcast_convert_type(jnp.where(key < 0, key ^ 0x7FFFFFFF, key), F32)

    kf = float(ktop)
    n_adm = jnp.broadcast_to(jnp.asarray(k_lim, F32), (1, tq))
    all_adm = n_adm <= kf
    c0 = count(lambda blk, k0: blk >= 0.0)
    cand0 = jnp.where(c0 >= kf, 0, INT_MIN).astype(I32)
    cnt0 = jnp.where(c0 >= kf, c0, kf + 1.0)

    def settled(cnt):
        return jnp.min(jnp.where(jnp.logical_or(all_adm, cnt == kf), 1.0, 0.0)) > 0.0

    def bit_cond(st):
        bit, _, cnt = st
        return jnp.logical_and(bit >= 0, jnp.logical_not(settled(cnt)))

    def bit_body(st):
        bit, cand, cnt = st
        trial = cand | lax.shift_left(jnp.int32(1), bit)
        trial_f = as_score(trial)
        c = count(lambda blk, k0: blk >= trial_f)
        take = c >= kf
        return bit - 1, jnp.where(take, trial, cand), jnp.where(take, c, cnt)

    _, cand, cnt = lax.while_loop(bit_cond, bit_body, (jnp.int32(30), cand0, cnt0))
    thr = jnp.where(all_adm, -jnp.inf, as_score(cand))
    pcut0 = jnp.where(all_adm, 0, lpad).astype(I32)
    pcut_ref[...] = pcut0

    @pl.when(jnp.logical_not(settled(cnt)))
    def _():
        tie = jnp.logical_not(jnp.logical_or(all_adm, cnt == kf))
        need = kf - count(lambda blk, k0: blk > thr)
        nbits = max(1, (lpad - 1).bit_length())

        def pbody(it, q):
            trial = q | lax.shift_left(jnp.int32(1), nbits - 1 - it)
            c = count(lambda blk, k0: jnp.logical_and(blk == thr, (k0 + row_iota_cc) < trial))
            return jnp.where(c < need, trial, q)

        q = lax.fori_loop(0, nbits, pbody, jnp.zeros((1, tq), I32))
        pcut_ref[...] = jnp.where(tie, q + 1, pcut0)

    pcut = pcut_ref[...]

    def write_chunk(c, carry):
        k0 = pl.multiple_of(c * ch, ch)
        blk = key_ref[pl.ds(k0, ch), :]
        sel = jnp.logical_or(blk > thr, jnp.logical_and(blk == thr, (k0 + row_iota) < pcut))
        bias_ref[pl.ds(k0, ch), :] = jnp.where(sel, 0.0, MASK_NEG).astype(bias_ref.dtype)
        return carry

    lax.fori_loop(0, nch, write_chunk, 0)

    def fill_chunk(c, carry):
        k0 = pl.multiple_of(c * ch, ch)
        bias_ref[pl.ds(k0, ch), :] = jnp.full((ch, tq), MASK_NEG, bias_ref.dtype)
        return carry

    lax.fori_loop(nch, nch_all, fill_chunk, 0)


def _select(ki, qit, wit, *, l_valid, causal, ktop, idx_dim, t_len=None):
    nb, lpad, _ = ki.shape
    t_len = qit.shape[2] if t_len is None else t_len
    tq = _pick(t_len, 256)
    ch = _pick(lpad, 512)
    cc = next(c for c in (4 * ch, 3 * ch, 2 * ch, ch) if lpad % c == 0)
    assert ch >= ktop and lpad % ch == 0 and ch % COUNT_ROWS == 0
    kern = functools.partial(_sel_kernel, tq=tq, ch=ch, cc=cc, lpad=lpad, l_valid=l_valid,
                             causal=causal, ktop=ktop, idx_dim=idx_dim)
    return pl.pallas_call(
        kern,
        grid=(nb, t_len // tq),
        in_specs=[pl.BlockSpec((None, lpad, idx_dim), lambda b, i: (b, 0, 0)),
                  pl.BlockSpec((None, qit.shape[1], tq), lambda b, i: (b, 0, i)),
                  pl.BlockSpec((None, wit.shape[1], tq), lambda b, i: (b, 0, i))],
        out_specs=pl.BlockSpec((None, lpad, tq), lambda b, i: (b, 0, i)),
        out_shape=jax.ShapeDtypeStruct((nb, lpad, t_len), BF16),
        scratch_shapes=[pltpu.VMEM((lpad, tq), F32), pltpu.VMEM((1, tq), I32)],
        compiler_params=_cparams(("parallel", "arbitrary")),
        name="index_select",
    )(ki, qit, wit)


def _att_kernel(it_ref, jt_ref, fin_ref, q_ref, k_ref, vt_ref, b_ref, o_ref, m_ref, l_ref, acc_ref, s_ref, p_ref, *,
                tq, tk, n_kv, hd, rep):
    step = pl.program_id(1)

    @pl.when(jt_ref[step] == 0)
    def _():
        m_ref[...] = jnp.full(m_ref.shape, MASK_NEG, F32)
        l_ref[...] = jnp.zeros(l_ref.shape, F32)
        acc_ref[...] = jnp.zeros(acc_ref.shape, F32)

    def compute():
        bias = b_ref[...].astype(F32)
        bias = jnp.concatenate([bias] * rep, axis=1)
        r = rep * tq
        nblk = tk // ATT_ROWS
        def scores(g):
            k = k_ref[:, g * hd:(g + 1) * hd]
            s_ref[g] = jnp.dot(k, q_ref[g], preferred_element_type=F32) + bias

        def softmax_update(g):
            def max_body(b, m8):
                blk = s_ref[g, pl.ds(pl.multiple_of(b * ATT_ROWS, ATT_ROWS), ATT_ROWS), :]
                return jnp.maximum(m8, jnp.max(blk.reshape(ATT_ROWS // 8, 8, r), axis=0))

            m8 = lax.fori_loop(0, nblk, max_body, jnp.full((8, r), MASK_NEG, F32), unroll=True)
            m_prev = m_ref[g]
            m_new = jnp.maximum(m_prev, jnp.max(m8, axis=0, keepdims=True))
            alpha = jnp.exp2(m_prev - m_new)

            def exp_body(b, l8):
                r0 = pl.multiple_of(b * ATT_ROWS, ATT_ROWS)
                p = jnp.exp2(s_ref[g, pl.ds(r0, ATT_ROWS), :] - m_new)
                p_ref[g, pl.ds(r0, ATT_ROWS), :] = p.astype(BF16)
                return l8 + jnp.sum(p.reshape(ATT_ROWS // 8, 8, r), axis=0)

            l8 = lax.fori_loop(0, nblk, exp_body, jnp.zeros((8, r), F32), unroll=True)
            l_ref[g] = alpha * l_ref[g] + jnp.sum(l8, axis=0, keepdims=True)
            m_ref[g] = m_new
            return alpha

        def weighted_values(g, alpha):
            pv = jnp.dot(vt_ref[g * hd:(g + 1) * hd, :], p_ref[g], preferred_element_type=F32)
            acc_ref[g] = alpha * acc_ref[g] + pv

        scores(0)
        for g in range(n_kv):
            if g + 1 < n_kv:
                scores(g + 1)
            weighted_values(g, softmax_update(g))

    compute()

    @pl.when(fin_ref[step] == 1)
    def _():
        for g in range(n_kv):
            o_t = (acc_ref[g] / l_ref[g]).T
            for rr in range(rep):
                h = g * rep + rr
                o_ref[:, h * hd:(h + 1) * hd] = o_t[rr * tq:(rr + 1) * tq, :].astype(o_ref.dtype)


def _attention(qt, k, vt, bias_t, *, tq, causal, nt=None):
    nb, nt_all, n_kv, hd, r = qt.shape
    nt = nt_all if nt is None else nt
    rep = r // tq
    lpad = k.shape[1]
    tk = _pick(lpad, 512)
    nk = lpad // tk
    n_keys = [min(nk, ((i + 1) * tq - 1) // tk + 1) if causal else nk for i in range(nt)]
    it = jnp.asarray([i for i in range(nt) for _ in range(n_keys[i])], I32)
    jt = jnp.asarray([j for i in range(nt) for j in range(n_keys[i])], I32)
    fin = jnp.asarray([int(j == n_keys[i] - 1) for i in range(nt) for j in range(n_keys[i])], I32)
    kern = functools.partial(_att_kernel, tq=tq, tk=tk, n_kv=n_kv, hd=hd, rep=rep)
    grid_spec = pltpu.PrefetchScalarGridSpec(
        num_scalar_prefetch=3,
        grid=(nb, int(it.shape[0])),
        in_specs=[pl.BlockSpec((None, None, n_kv, hd, r), lambda b, s, it, jt, fin: (b, it[s], 0, 0, 0)),
                  pl.BlockSpec((None, tk, n_kv * hd), lambda b, s, it, jt, fin: (b, jt[s], 0)),
                  pl.BlockSpec((None, n_kv * hd, tk), lambda b, s, it, jt, fin: (b, 0, jt[s])),
                  pl.BlockSpec((None, tk, tq), lambda b, s, it, jt, fin: (b, jt[s], it[s]))],
        out_specs=pl.BlockSpec((tq, n_kv * rep * hd), lambda b, s, it, jt, fin: (b * nt + it[s], 0)),
        scratch_shapes=[pltpu.VMEM((n_kv, 1, r), F32),
                        pltpu.VMEM((n_kv, 1, r), F32),
                        pltpu.VMEM((n_kv, hd, r), F32),
                        pltpu.VMEM((n_kv, tk, r), F32),
                        pltpu.VMEM((n_kv, tk, r), BF16)],
    )
    return pl.pallas_call(
        kern,
        grid_spec=grid_spec,
        out_shape=jax.ShapeDtypeStruct((nb * nt * tq, n_kv * rep * hd), BF16),
        compiler_params=_cparams(("parallel", "arbitrary")),
        name="sparse_attention",
    )(it, jt, fin, qt, k, vt, bias_t)


def _mematt_kernel(q_ref, k_ref, v_ref, o_ref, *, n_heads, hd):
    scale = hd ** -0.5
    for h in range(n_heads):
        q = q_ref[:, h * hd:(h + 1) * hd]
        k = k_ref[:, h * hd:(h + 1) * hd].astype(BF16)
        v = v_ref[:, h * hd:(h + 1) * hd].astype(BF16)
        s = lax.dot_general(q, k, (((1,), (1,)), ((), ())), preferred_element_type=F32) * scale
        m = jnp.max(s, axis=-1, keepdims=True)
        p = jnp.exp(s - m)
        l = jnp.sum(p, axis=-1, keepdims=True)
        o = jnp.dot(p.astype(BF16), v, preferred_element_type=F32) / l
        o_ref[:, h * hd:(h + 1) * hd] = o.astype(o_ref.dtype)


def _mem_attention(mq, mk, mv, *, row_off, nb, t_len, n_heads, hd):
    tq = _pick(t_len, 512)
    nt = t_len // tq
    off = row_off // tq
    n_mem = mk.shape[1]
    w = n_heads * hd
    return pl.pallas_call(
        functools.partial(_mematt_kernel, n_heads=n_heads, hd=hd),
        grid=(nb, nt),
        in_specs=[pl.BlockSpec((tq, w), lambda b, i: (off + b * nt + i, 0)),
                  pl.BlockSpec((None, n_mem, w), lambda b, i: (b, 0, 0)),
                  pl.BlockSpec((None, n_mem, w), lambda b, i: (b, 0, 0))],
        out_specs=pl.BlockSpec((tq, w), lambda b, i: (b * nt + i, 0)),
        out_shape=jax.ShapeDtypeStruct((nb * t_len, w), BF16),
        compiler_params=_cparams(("parallel", "parallel")),
        name="memory_attention",
    )(mq, mk, mv)


def _router_kernel(x_ref, g_ref, w_ref, b_ref, xn_ref, idx_ref, p_ref):
    x = x_ref[...]
    ms = jnp.mean(x * x, axis=-1, keepdims=True)
    xn = x * lax.rsqrt(ms + EPS) * g_ref[...]
    xn_ref[...] = xn
    logits = jnp.dot(xn.astype(BF16), w_ref[...], preferred_element_type=F32) + b_ref[...]
    lane = lax.broadcasted_iota(I32, logits.shape, 1)
    lane_f = lane.astype(F32)
    vals, idxs = [], []
    l = logits
    for _ in range(TOP_K):
        m = jnp.max(l, axis=-1, keepdims=True)
        ix = jnp.min(jnp.where(l == m, lane_f, float(LANES)), axis=-1, keepdims=True)
        vals.append(m)
        idxs.append(ix)
        l = jnp.where(lane_f == ix, -jnp.inf, l)
    es = [jnp.exp(v - vals[0]) for v in vals]
    den = es[0]
    for e in es[1:]:
        den = den + e
    idx_out = jnp.zeros(logits.shape, F32)
    p_out = jnp.zeros(logits.shape, F32)
    for r in range(TOP_K):
        idx_out = jnp.where(lane == r, idxs[r], idx_out)
        p_out = jnp.where(lane == r, es[r] / den, p_out)
    idx_ref[...] = idx_out.astype(I32)
    p_ref[...] = p_out


def _router(x1, g, w_router, b_router):
    n, d = x1.shape
    ne = w_router.shape[1]
    tm = _pick(n, 512)
    w_pad = jnp.zeros((d, LANES), BF16).at[:, :ne].set(w_router.astype(BF16))
    b_pad = jnp.full((1, LANES), MASK_NEG, F32).at[0, :ne].set(b_router)
    return pl.pallas_call(
        _router_kernel,
        grid=(n // tm,),
        in_specs=[pl.BlockSpec((tm, d), lambda i: (i, 0)),
                  pl.BlockSpec((1, d), lambda i: (0, 0)),
                  pl.BlockSpec((d, LANES), lambda i: (0, 0)),
                  pl.BlockSpec((1, LANES), lambda i: (0, 0))],
        out_specs=[pl.BlockSpec((tm, d), lambda i: (i, 0)),
                   pl.BlockSpec((tm, LANES), lambda i: (i, 0)),
                   pl.BlockSpec((tm, LANES), lambda i: (i, 0))],
        out_shape=[jax.ShapeDtypeStruct((n, d), F32),
                   jax.ShapeDtypeStruct((n, LANES), I32),
                   jax.ShapeDtypeStruct((n, LANES), F32)],
        compiler_params=_cparams(("parallel",)),
        name="router",
    )(x1, g.reshape(1, d), w_pad, b_pad)


def _row_copy(src, dst, src_row, dst_row, sem):
    return pltpu.make_async_copy(src.at[pl.ds(src_row, 1)], dst.at[pl.ds(dst_row, 1)], sem)


def _moe_kernel(te_ref, tp_ref, tr_ref, nu_ref, ord_ref, x_hbm, wg_ref, wu_ref, bg_ref, bu_ref, wd_ref, bd_ref,
                y_hbm, xf_ref, xb_ref, acc_ref, gsem, ssem, *, tm, nf):
    t = pl.program_id(0)
    f = pl.program_id(1)
    shift = TOP_K.bit_length() - 1

    def scatter(tile, op):
        base = tp_ref[tile]

        def body(r, c):
            cp = _row_copy(acc_ref, y_hbm, r, ord_ref[base + r], ssem)
            cp.start() if op == "start" else cp.wait()
            return c

        lax.fori_loop(0, tr_ref[tile], body, 0)

    @pl.when(t < nu_ref[0])
    def _():
        @pl.when(f == 0)
        def _():
            base = tp_ref[t]

            def issue(r, c):
                _row_copy(x_hbm, xf_ref, lax.shift_right_logical(ord_ref[base + r], shift), r, gsem).start()
                return c

            lax.fori_loop(0, tm, issue, 0)

            def wait(r, c):
                _row_copy(x_hbm, xf_ref, lax.shift_right_logical(ord_ref[base + r], shift), r, gsem).wait()
                return c

            lax.fori_loop(0, tm, wait, 0)
            xb_ref[...] = xf_ref[...].astype(BF16)

        xb = xb_ref[...]
        hg = jnp.dot(xb, wg_ref[...].astype(BF16), preferred_element_type=F32) + bg_ref[...]
        hu = jnp.dot(xb, wu_ref[...].astype(BF16), preferred_element_type=F32) + bu_ref[...]
        gate = jnp.minimum(hg, SWIGLU_LIMIT)
        up = jnp.clip(hu, -SWIGLU_LIMIT, SWIGLU_LIMIT)
        act = (up + 1.0) * gate * jax.nn.sigmoid(SWIGLU_ALPHA * gate)
        contrib = jnp.dot(act.astype(BF16), wd_ref[...].astype(BF16), preferred_element_type=F32)

        @pl.when(f == 0)
        def _():
            @pl.when(t > 0)
            def _():
                scatter(t - 1, "wait")

            acc_ref[...] = contrib + bd_ref[...]

        @pl.when(f > 0)
        def _():
            acc_ref[...] += contrib

        @pl.when(f == nf - 1)
        def _():
            scatter(t, "start")

            @pl.when(t == nu_ref[0] - 1)
            def _():
                scatter(t, "wait")


def _moe_experts(xn, tile_expert, tile_pos, tile_rows, n_used, order, w_up_gate, b_up_gate, w_down, b_down, *, tm, fc):
    n, d = xn.shape
    ne, _, ff2 = w_up_gate.shape
    ff = ff2 // 2
    nf = ff // fc
    mt = tile_expert.shape[0]

    def fe(t, f, nu):
        return jnp.where(t < nu[0], f, nf - 1)

    grid_spec = pltpu.PrefetchScalarGridSpec(
        num_scalar_prefetch=5,
        grid=(mt, nf),
        in_specs=[
            pl.BlockSpec(memory_space=pl.ANY),
            pl.BlockSpec((None, d, fc), lambda t, f, te, tp, tr, nu, o: (te[t], 0, fe(t, f, nu))),
            pl.BlockSpec((None, d, fc), lambda t, f, te, tp, tr, nu, o: (te[t], 0, nf + fe(t, f, nu))),
            pl.BlockSpec((None, 1, fc), lambda t, f, te, tp, tr, nu, o: (te[t], 0, fe(t, f, nu))),
            pl.BlockSpec((None, 1, fc), lambda t, f, te, tp, tr, nu, o: (te[t], 0, nf + fe(t, f, nu))),
            pl.BlockSpec((None, fc, d), lambda t, f, te, tp, tr, nu, o: (te[t], fe(t, f, nu), 0)),
            pl.BlockSpec((None, 1, d), lambda t, f, te, tp, tr, nu, o: (te[t], 0, 0)),
        ],
        out_specs=pl.BlockSpec(memory_space=pl.ANY),
        scratch_shapes=[pltpu.VMEM((tm, d), F32), pltpu.VMEM((tm, d), BF16), pltpu.VMEM((tm, d), F32),
                        pltpu.SemaphoreType.DMA(()), pltpu.SemaphoreType.DMA(())],
    )
    return pl.pallas_call(
        functools.partial(_moe_kernel, tm=tm, nf=nf),
        grid_spec=grid_spec,
        out_shape=jax.ShapeDtypeStruct((n * TOP_K, d), F32),
        compiler_params=_cparams(("arbitrary", "arbitrary")),
        name="moe_experts",
    )(tile_expert, tile_pos, tile_rows, n_used, order, xn, w_up_gate, w_up_gate,
      b_up_gate.reshape(ne, 1, ff2), b_up_gate.reshape(ne, 1, ff2), w_down, b_down.reshape(ne, 1, d))


def _combine_kernel(x_ref, p_ref, g_ref, y_ref, o_ref):
    d = x_ref.shape[1]
    p = p_ref[...]
    x = x_ref[...]
    for k in range(TOP_K):
        x = x + p[:, k:k + 1] * y_ref[:, k * d:(k + 1) * d]
    ms = jnp.mean(x * x, axis=-1, keepdims=True)
    o_ref[...] = x * lax.rsqrt(ms + EPS) * g_ref[...]


def _combine(x1, probs, y, g_final):
    n, d = x1.shape
    tm = _pick(n, 256)
    return pl.pallas_call(
        _combine_kernel,
        grid=(n // tm,),
        in_specs=[pl.BlockSpec((tm, d), lambda i: (i, 0)),
                  pl.BlockSpec((tm, LANES), lambda i: (i, 0)),
                  pl.BlockSpec((1, d), lambda i: (0, 0)),
                  pl.BlockSpec((tm, TOP_K * d), lambda i: (i, 0))],
        out_specs=pl.BlockSpec((tm, d), lambda i: (i, 0)),
        out_shape=jax.ShapeDtypeStruct((n, d), F32),
        compiler_params=_cparams(("parallel",)),
        name="moe_combine",
    )(x1, probs, g_final.reshape(1, d), y)


def _rope_tables(pos, dim):
    half = dim // 2
    inv_freq = ROPE_THETA ** (-jnp.arange(half, dtype=F32) / half)
    ang = pos.astype(F32)[:, None] * inv_freq[None, :]
    cos, sin = jnp.cos(ang), jnp.sin(ang)
    reps = LANES // dim
    cos_t = jnp.tile(jnp.concatenate([cos, cos], axis=-1), (1, reps))
    sin_t = jnp.tile(jnp.concatenate([-sin, sin], axis=-1), (1, reps))
    return cos_t, sin_t


def _pad_len(l, mult):
    return (l + mult - 1) // mult * mult


def kernel(x_prompt, x_sample, cache_attn_k, cache_attn_v, cache_idx_k, state_conv, cache_mem_k, cache_mem_v, mem_prompt, g_mix, w_comb, conv_w, w_conv_proj, w_attn_proj, g_mem, w_mem_kv, w_mem_proj, w_merge_out, g_moe, w_router, b_router, w_up_gate, b_up_gate, w_down, b_down, g_final):
    bp, t_p, d = x_prompt.shape
    bs, t_s, _ = x_sample.shape
    assert bp == 1 and g_mix.shape[0] == 1
    past = cache_attn_k.shape[2]
    n_kv, hd = cache_attn_k.shape[3], cache_attn_k.shape[4]
    idx_dim = cache_idx_k.shape[3]
    cw = conv_w.shape[2]
    attn_q = w_attn_proj.shape[1]
    n_heads = attn_q // hd
    attn_kv = n_kv * hd
    idx_q = IDX_HEADS * idx_dim
    n_mem, mem_heads, mem_hd = cache_mem_k.shape[2], cache_mem_k.shape[3], cache_mem_k.shape[4]
    mem_q = mem_heads * mem_hd
    n_p, n_s = bp * t_p, bs * t_s
    n = n_p + n_s

    o_cin, o_cb, o_cc = 0, cw, 2 * cw
    o_q = 3 * cw
    o_k = o_q + attn_q
    o_v = o_k + attn_kv
    o_qi = o_v + attn_kv
    o_ki = o_qi + idx_q
    o_wi = o_ki + idx_dim
    o_mq = o_wi + IDX_HEADS
    o_g = o_mq + mem_q
    wc = w_comb[0]
    w_main = wc[:, :o_ki].astype(BF16)
    w_kw = jnp.zeros((d, LANES), BF16).at[:, :idx_dim + IDX_HEADS].set(wc[:, o_ki:o_mq].astype(BF16))
    w_mq = wc[:, o_mq:o_g].astype(BF16)
    w_gates = wc[:, o_g:].astype(BF16)

    x = jnp.concatenate([x_prompt.reshape(n_p, d), x_sample.reshape(n_s, d)], axis=0)
    pos = jnp.concatenate([jnp.arange(t_p, dtype=I32),
                           jnp.tile(past + jnp.arange(t_s, dtype=I32), bs)])
    cos_h, sin_h = _rope_tables(pos, hd)
    cos_i, sin_i = _rope_tables(pos, idx_dim)
    tm = _pick(n, 512)
    rowtab = lambda a: (a, (tm, LANES), lambda i, j: (i, 0))

    xn = _rmsnorm(x, g_mix[0], BF16)

    tn_c = _pick(cw, 512)
    nbc = cw // tn_c

    def ep_conv(accs, extras, outs):
        outs[0][...] = accs[2] * accs[0]
        outs[1][...] = accs[1]

    u, cb = _mm(xn, [(w_main, 0), (w_main, nbc), (w_main, 2 * nbc)], ep_conv,
                [(cw, F32, tn_c, None), (cw, F32, tn_c, None)], tn=tn_c, name="proj_conv")

    attn_scale = hd ** -0.5 * LOG2E
    tn_q = _pick(attn_q, 1024)

    rep = n_heads // n_kv
    tq_p = _pick(t_p, 256)
    fused_t = tm % tq_p == 0 and n_p % tm == 0 and t_p % LANES == 0 and tn_q % (rep * hd) == 0
    qpt = tm // tq_p
    gpc = tn_q // (rep * hd)

    def ep_q(accs, extras, outs):
        pieces = _rope_groups(accs[0], extras[0][...], extras[1][...], hd // 2)
        for c, p in enumerate(pieces):
            p = p * attn_scale
            outs[0][:, c * LANES:(c + 1) * LANES] = p.astype(BF16)
            if fused_t:
                p_t = p.T
                g_loc, rr = divmod(c, rep)
                for t in range(qpt):
                    outs[1][t, g_loc, :, rr * tq_p:(rr + 1) * tq_p] = p_t[:, t * tq_p:(t + 1) * tq_p].astype(BF16)

    q_outs = [(attn_q, BF16, tn_q, None)]
    if fused_t:
        q_outs.append(((n // tq_p, n_kv, hd, rep * tq_p), BF16, (qpt, gpc, hd, rep * tq_p),
                       lambda i, j: (i, j, 0, 0), None))
    q_res = _mm(xn, [(w_main, o_q // tn_q)], ep_q, q_outs,
                extras=[rowtab(cos_h), rowtab(sin_h)], tn=tn_q, name="proj_q")
    q_r = q_res[0]

    def ep_k(accs, extras, outs):
        pieces = _rope_groups(accs[0], extras[0][...], extras[1][...], hd // 2)
        for c, p in enumerate(pieces):
            outs[0][:, c * LANES:(c + 1) * LANES] = p
            outs[1][:, c * LANES:(c + 1) * LANES] = p.astype(BF16)

    k_f, k_b = _mm(xn, [(w_main, o_k // attn_kv)], ep_k,
                   [(attn_kv, F32, attn_kv, None), (attn_kv, BF16, attn_kv, None)],
                   extras=[rowtab(cos_h), rowtab(sin_h)], tn=attn_kv, name="proj_k")

    def ep_v(accs, extras, outs):
        outs[0][...] = accs[0]
        outs[1][...] = accs[0].astype(BF16)
        if fused_t:
            outs[2][...] = accs[0].T.astype(BF16)

    v_outs = [(attn_kv, F32, attn_kv, None), (attn_kv, BF16, attn_kv, None)]
    if fused_t:
        v_outs.append(((attn_kv, n), BF16, (attn_kv, tm), lambda i, j: (0, i), None))
    v_res = _mm(xn, [(w_main, o_v // attn_kv)], ep_v, v_outs, tn=attn_kv, name="proj_v")
    v_f, v_b = v_res[0], v_res[1]

    tn_i = _pick(idx_q, 1024)

    fused_i = fused_t and tn_i == idx_q

    def ep_qi(accs, extras, outs):
        pieces = _rope_groups(accs[0], extras[0][...], extras[1][...], idx_dim // 2)
        for c, p in enumerate(pieces):
            outs[0][:, c * LANES:(c + 1) * LANES] = p.astype(BF16)
            if fused_i:
                outs[1][c * LANES:(c + 1) * LANES, :] = p.T.astype(BF16)

    qi_outs = [(idx_q, BF16, tn_i, None)]
    if fused_i:
        qi_outs.append(((idx_q, n), BF16, (idx_q, tm), lambda i, j: (0, i), None))
    qi_res = _mm(xn, [(w_main, o_qi // tn_i)], ep_qi, qi_outs,
                 extras=[rowtab(cos_i), rowtab(sin_i)], tn=tn_i, name="proj_qi")
    qi_r = qi_res[0]

    wi_scale = IDX_HEADS ** -0.5

    def ep_kw(accs, extras, outs):
        a = accs[0]
        (roped,) = _rope_groups(a, extras[0][...], extras[1][...], idx_dim // 2)
        lane = lax.broadcasted_iota(I32, a.shape, 1)
        outs[0][...] = jnp.where(lane < idx_dim, roped, a * wi_scale)

    (kw,) = _mm(xn, [(w_kw, 0)], ep_kw, [(LANES, F32, LANES, None)],
                extras=[rowtab(cos_i), rowtab(sin_i)], tn=LANES, name="proj_ki_wi")
    ki_f = kw[:, :idx_dim]
    wi = kw[:, idx_dim:idx_dim + IDX_HEADS]

    def ep_cast(accs, extras, outs):
        outs[0][...] = accs[0].astype(outs[0].dtype)

    tn_m = _pick(mem_q, 1024)
    (mq,) = _mm(xn, [(w_mq, 0)], ep_cast, [(mem_q, BF16, tn_m, None)], tn=tn_m, name="proj_mq")

    def ep_sig(accs, extras, outs):
        outs[0][...] = jax.nn.sigmoid(accs[0])

    tn_g = _pick(d, 1024)
    (sg,) = _mm(xn, [(w_gates, 0)], ep_sig, [(3 * d, F32, tn_g, None)], tn=tn_g, name="proj_gates")

    conv_p, hist_p = _short_conv(u, cb, jnp.zeros((bp, 2, cw), F32), conv_w[0], 0, bp, t_p)
    conv_s, hist_s = _short_conv(u, cb, state_conv[0], conv_w[0], n_p, bs, t_s)
    conv_pre = jnp.concatenate([conv_p, conv_s], axis=0)

    ki_b = ki_f.astype(BF16)

    def to_qt(q, nb, t_len, tq):
        q = q.reshape(nb, t_len // tq, tq, n_kv, rep, hd)
        return jnp.transpose(q, (0, 1, 3, 5, 4, 2)).reshape(nb, t_len // tq, n_kv, hd, rep * tq)

    l_p = t_p
    lpad_p = _pad_len(l_p, LANES)
    pad_p = lpad_p - l_p
    qit_p = qi_res[1][None] if fused_i else qi_r[:n_p].T[None]
    bias_p = _select(jnp.pad(ki_b[:n_p], ((0, pad_p), (0, 0)))[None], qit_p, wi.T[None], t_len=t_p,
                     l_valid=l_p, causal=True, ktop=min(TOPK_MAX, l_p // 4), idx_dim=idx_dim)
    kp = jnp.pad(k_b[:n_p], ((0, pad_p), (0, 0)))[None]
    if fused_t:
        qt_p, vtp = q_res[1][None], v_res[2][None]
    else:
        qt_p, vtp = to_qt(q_r[:n_p], bp, t_p, tq_p), jnp.pad(v_b[:n_p].T, ((0, 0), (0, pad_p)))[None]
    attn_p = _attention(qt_p, kp, vtp, bias_p, tq=tq_p, causal=True, nt=t_p // tq_p)

    l_s = past + t_s
    lpad_s = _pad_len(l_s, 512)
    pad_s = lpad_s - l_s
    ki_all = jnp.concatenate([cache_idx_k[0].astype(BF16), ki_b[n_p:].reshape(bs, t_s, idx_dim)], axis=1)
    bias_s = _select(jnp.pad(ki_all, ((0, 0), (0, pad_s), (0, 0))),
                     jnp.swapaxes(qi_r[n_p:].reshape(bs, t_s, idx_q), 1, 2),
                     jnp.swapaxes(wi[n_p:].reshape(bs, t_s, IDX_HEADS), 1, 2),
                     l_valid=l_s, causal=False, ktop=min(TOPK_MAX, l_s // 4), idx_dim=idx_dim)
    ks = jnp.concatenate([cache_attn_k[0].reshape(bs, past, attn_kv).astype(BF16),
                          k_b[n_p:].reshape(bs, t_s, attn_kv)], axis=1)
    vs = jnp.concatenate([cache_attn_v[0].reshape(bs, past, attn_kv).astype(BF16),
                          v_b[n_p:].reshape(bs, t_s, attn_kv)], axis=1)
    ks = jnp.pad(ks, ((0, 0), (0, pad_s), (0, 0)))
    vts = jnp.pad(jnp.swapaxes(vs, 1, 2), ((0, 0), (0, 0), (0, pad_s)))
    attn_s = _attention(to_qt(q_r[n_p:], bs, t_s, t_s), ks, vts, bias_s, tq=t_s, causal=False)
    attn = jnp.concatenate([attn_p, attn_s], axis=0)

    memn = _rmsnorm(mem_prompt.reshape(bp * n_mem, d), g_mem[0], BF16)
    tn_kv = _pick(2 * mem_q, 1024)

    def ep_f32(accs, extras, outs):
        outs[0][...] = accs[0]

    (mem_kv,) = _mm(memn, [(w_mem_kv[0].astype(BF16), 0)], ep_f32, [(2 * mem_q, F32, tn_kv, None)],
                    tn=tn_kv, name="proj_mem_kv")
    mk_p = mem_kv[:, :mem_q].reshape(bp, n_mem, mem_q)
    mv_p = mem_kv[:, mem_q:].reshape(bp, n_mem, mem_q)
    memo_p = _mem_attention(mq, mk_p, mv_p, row_off=0, nb=bp, t_len=t_p, n_heads=mem_heads, hd=mem_hd)
    memo_s = _mem_attention(mq, cache_mem_k[0].reshape(bs, n_mem, mem_q), cache_mem_v[0].reshape(bs, n_mem, mem_q),
                            row_off=n_p, nb=bs, t_len=t_s, n_heads=mem_heads, hd=mem_hd)
    memo = jnp.concatenate([memo_p, memo_s], axis=0)

    ngb = d // tn_g

    def gate_extra(which):
        return (sg, (tm, tn_g), functools.partial(lambda i, j, o: (i, o + j), o=which * ngb))

    def ep_gate0(accs, extras, outs):
        outs[0][...] = extras[0][...] * accs[0]

    def ep_gate(accs, extras, outs):
        outs[0][...] = (extras[1][...] + extras[0][...] * accs[0]).astype(outs[0].dtype)

    prev = lambda a: (a, (tm, tn_g), lambda i, j: (i, j))
    (m1,) = _mm(conv_pre, [(w_conv_proj[0].astype(BF16), 0)], ep_gate0, [(d, F32, tn_g, None)],
                extras=[gate_extra(0)], tn=tn_g, name="merge_conv")
    (m2,) = _mm(attn, [(w_attn_proj[0].astype(BF16), 0)], ep_gate, [(d, F32, tn_g, None)],
                extras=[gate_extra(1), prev(m1)], tn=tn_g, name="merge_attn")
    (m3,) = _mm(memo, [(w_mem_proj[0].astype(BF16), 0)], ep_gate, [(d, BF16, tn_g, None)],
                extras=[gate_extra(2), prev(m2)], tn=tn_g, name="merge_mem")

    def ep_res(accs, extras, outs):
        outs[0][...] = extras[0][...] + accs[0]

    (x1,) = _mm(m3, [(w_merge_out[0].astype(BF16), 0)], ep_res, [(d, F32, tn_g, None)],
                extras=[prev(x)], tn=tn_g, name="merge_out")

    xn2, ridx, rp = _router(x1, g_moe[0], w_router[0], b_router[0])
    ne = w_router.shape[2]
    ff = w_down.shape[2]
    nk = n * TOP_K
    tme = MOE_TILE if nk >= 16 * MOE_TILE else 128
    e_flat = ridx[:, :TOP_K].reshape(nk)
    order = jnp.argsort(e_flat, stable=True).astype(I32)
    experts = jnp.arange(ne, dtype=I32)
    counts = jnp.sum((e_flat[:, None] == experts[None, :]).astype(I32), axis=0)
    tiles_per = (counts + tme - 1) // tme
    tile_end = jnp.cumsum(tiles_per)
    tile_first = tile_end - tiles_per
    cstart = jnp.cumsum(counts) - counts
    n_used = tile_end[-1].astype(I32)
    mt = nk // tme + ne
    tile_ids = jnp.arange(mt, dtype=I32)
    owner = jnp.logical_and(tile_ids[:, None] >= tile_first[None, :], tile_ids[:, None] < tile_end[None, :])
    pick = lambda v: jnp.sum(jnp.where(owner, v[None, :], 0), axis=1).astype(I32)
    k_in = tile_ids - pick(tile_first)
    tile_pos = pick(cstart) + k_in * tme
    tile_rows = jnp.clip(pick(counts) - k_in * tme, 0, tme)
    last_e = jnp.sum(jnp.where(tile_ids == n_used - 1, pick(experts), 0))
    tile_expert = jnp.where(tile_ids < n_used, pick(experts), last_e).astype(I32)
    order_pad = jnp.concatenate([order, jnp.zeros((tme,), I32)])
    y_assign = _moe_experts(xn2, tile_expert, tile_pos, tile_rows, n_used.reshape(1), order_pad,
                            w_up_gate[0], b_up_gate[0], w_down[0], b_down[0], tm=tme, fc=_pick(ff, 256))
    y = _combine(x1, rp, y_assign.reshape(n, TOP_K * d), g_final)

    y_prompt = y[:n_p].reshape(bp, t_p, d)
    y_sample = y[n_p:].reshape(bs, t_s, d)
    new_k_p = k_f[:n_p].reshape(1, bp, t_p, n_kv, hd)
    new_v_p = v_f[:n_p].reshape(1, bp, t_p, n_kv, hd)
    new_ki_p = ki_f[:n_p].reshape(1, bp, t_p, idx_dim)
    new_k_s = k_f[n_p:].reshape(1, bs, t_s, n_kv, hd)
    new_v_s = v_f[n_p:].reshape(1, bs, t_s, n_kv, hd)
    new_ki_s = ki_f[n_p:].reshape(1, bs, t_s, idx_dim)
    return (y_prompt, y_sample, new_k_p, new_v_p, new_ki_p, hist_p[None],
            mk_p.reshape(1, bp, n_mem, mem_heads, mem_hd), mv_p.reshape(1, bp, n_mem, mem_heads, mem_hd),
            new_k_s, new_v_s, new_ki_s, hist_s[None])
```

```python
import functools

import jax
import jax.numpy as jnp
from jax import lax
from jax.experimental import pallas as pl
from jax.experimental.pallas import tpu as pltpu

F32 = jnp.float32
BF16 = jnp.bfloat16
I32 = jnp.int32

CHUNK = 64
IDX_HEADS = 16
TOPK_MAX = 256
TOP_K = 4
SWIGLU_LIMIT = 7.0
SWIGLU_ALPHA = 1.702
ROPE_THETA = 10000.0
EPS = 1e-6

LANES = 128
VMEM_LIMIT = 56 * 1024 * 1024
LOG2E = 1.4426950408889634
ROW_UNROLL = 8
ATT_ROWS = 32
COUNT_ROWS = 64
MOE_TILE = 1024
MASK_NEG = -1e30
INT_MIN = -2147483648
NEGINF_KEY = -2139095041


def _cparams(sem):
    return pltpu.CompilerParams(dimension_semantics=sem, vmem_limit_bytes=VMEM_LIMIT)


def _pick(n, pref):
    if n <= pref:
        return n
    t = pref
    while n % t:
        t //= 2
    return t


def _rms_kernel(x_ref, g_ref, o_ref):
    x = x_ref[...]
    ms = jnp.mean(x * x, axis=-1, keepdims=True)
    o_ref[...] = (x * lax.rsqrt(ms + EPS) * g_ref[...]).astype(o_ref.dtype)


def _rmsnorm(x, g, out_dtype):
    n, d = x.shape
    tm = _pick(n, 512)
    return pl.pallas_call(
        _rms_kernel,
        grid=(n // tm,),
        in_specs=[pl.BlockSpec((tm, d), lambda i: (i, 0)),
                  pl.BlockSpec((1, d), lambda i: (0, 0))],
        out_specs=pl.BlockSpec((tm, d), lambda i: (i, 0)),
        out_shape=jax.ShapeDtypeStruct((n, d), out_dtype),
        compiler_params=_cparams(("parallel",)),
        name="rmsnorm",
    )(x, g.reshape(1, d))


def _mm_kernel(epilogue, n_w, n_extra, x_ref, *refs):
    w_refs = refs[:n_w]
    extras = refs[n_w:n_w + n_extra]
    outs = refs[n_w + n_extra:]
    x = x_ref[...]
    accs = [jnp.dot(x, w[...], preferred_element_type=F32) for w in w_refs]
    epilogue(accs, extras, outs)


def _mm(x, ws, epilogue, outs, extras=(), *, tn, tm=512, name="mm"):
    n, k = x.shape
    tm = _pick(n, tm)
    ncol = outs[0][0] // outs[0][2]
    in_specs = [pl.BlockSpec((tm, k), lambda i, j: (i, 0))]
    args = [x]
    for w, off in ws:
        in_specs.append(pl.BlockSpec((k, tn), functools.partial(lambda i, j, o: (0, o + j), o=off)))
        args.append(w)
    for a, bs, im in extras:
        in_specs.append(pl.BlockSpec(bs, im))
        args.append(a)
    out_specs, out_shape = [], []
    for o in outs:
        if len(o) == 5:
            shape, dt, bs, im, _ = o
            out_specs.append(pl.BlockSpec(bs, im))
            out_shape.append(jax.ShapeDtypeStruct(shape, dt))
            continue
        width, dt, bw, im = o
        out_specs.append(pl.BlockSpec((tm, bw), im if im is not None else (lambda i, j: (i, j))))
        out_shape.append(jax.ShapeDtypeStruct((n, width), dt))
    res = pl.pallas_call(
        functools.partial(_mm_kernel, epilogue, len(ws), len(extras)),
        grid=(n // tm, ncol),
        in_specs=in_specs,
        out_specs=out_specs,
        out_shape=out_shape,
        compiler_params=_cparams(("parallel", "arbitrary")),
        name=name,
    )(*args)
    return res


def _rope_groups(acc, cos, sin, half):
    tn = acc.shape[1]
    lane = lax.broadcasted_iota(I32, (acc.shape[0], LANES), 1)
    pieces = []
    for c in range(tn // LANES):
        a = acc[:, c * LANES:(c + 1) * LANES]
        if 2 * half == LANES:
            partner = pltpu.roll(a, half, axis=1)
        else:
            first = (lane & (2 * half - 1)) < half
            partner = jnp.where(first, pltpu.roll(a, LANES - half, axis=1), pltpu.roll(a, half, axis=1))
        pieces.append(a * cos + partner * sin)
    return pieces


def _conv_kernel(u_ref, cb_ref, h_ref, w_ref, o_ref, nh_ref, carry_ref):
    t = pl.program_id(1)

    @pl.when(t == 0)
    def _():
        carry_ref[0:2, :] = h_ref[...]

    u = u_ref[...]
    tm = u.shape[0]
    h0 = carry_ref[0:1, :]
    h1 = carry_ref[1:2, :]
    row = lax.broadcasted_iota(I32, u.shape, 0)
    p1 = jnp.where(row == 0, h1, pltpu.roll(u, 1, axis=0))
    p2 = jnp.where(row == 0, h0, jnp.where(row == 1, h1, pltpu.roll(u, 2, axis=0)))
    w = w_ref[...]
    y = w[0:1, :] * p2 + w[1:2, :] * p1 + w[2:3, :] * u
    o_ref[...] = (cb_ref[...] * y).astype(o_ref.dtype)
    last = u[tm - 2:tm, :]
    carry_ref[0:2, :] = last
    nh_ref[...] = last


def _short_conv(u, cb, hist, conv_w, row_off, nb, t_len):
    c = u.shape[1]
    tm = _pick(t_len, 512)
    nt = t_len // tm
    off = row_off // tm
    rows = lambda b, t: (off + b * nt + t, 0)
    conv_pre, new_hist = pl.pallas_call(
        _conv_kernel,
        grid=(nb, nt),
        in_specs=[pl.BlockSpec((tm, c), rows),
                  pl.BlockSpec((tm, c), rows),
                  pl.BlockSpec((None, 2, c), lambda b, t: (b, 0, 0)),
                  pl.BlockSpec((3, c), lambda b, t: (0, 0))],
        out_specs=[pl.BlockSpec((tm, c), lambda b, t: (b * nt + t, 0)),
                   pl.BlockSpec((None, 2, c), lambda b, t: (b, 0, 0))],
        out_shape=[jax.ShapeDtypeStruct((nb * t_len, c), BF16),
                   jax.ShapeDtypeStruct((nb, 2, c), F32)],
        scratch_shapes=[pltpu.VMEM((8, c), F32)],
        compiler_params=_cparams(("arbitrary", "arbitrary")),
        name="short_conv",
    )(u, cb, hist, conv_w)
    return conv_pre, new_hist


def _sel_kernel(ki_ref, qit_ref, wit_ref, bias_ref, key_ref, pcut_ref, *,
                tq, ch, cc, lpad, l_valid, causal, ktop, idx_dim):
    i = pl.program_id(1)
    nch_all = lpad // ch
    if causal:
        nch = jnp.minimum(((i + 1) * tq + ch - 1) // ch, nch_all)
    else:
        nch = nch_all
    scale = idx_dim ** -0.5
    qit = qit_ref[...]
    w = wit_ref[...]
    qh = [qit[h * idx_dim:(h + 1) * idx_dim, :] for h in range(IDX_HEADS)]
    if causal:
        q_pos = i * tq + lax.broadcasted_iota(I32, (1, tq), 1)
        k_lim = jnp.minimum((q_pos // CHUNK + 1) * CHUNK, l_valid)
    else:
        k_lim = l_valid
    row_iota = lax.broadcasted_iota(I32, (ch, tq), 0)
    row_iota_cc = lax.broadcasted_iota(I32, (cc, tq), 0)

    def score_chunk(c, carry):
        k0 = pl.multiple_of(c * ch, ch)
        kc = ki_ref[pl.ds(k0, ch), :]
        acc = jnp.zeros((ch, tq), F32)
        for h in range(IDX_HEADS):
            d = jnp.dot(kc, qh[h], preferred_element_type=F32)
            acc = acc + jnp.maximum(d, 0.0) * w[h:h + 1, :]
        key_ref[pl.ds(k0, ch), :] = jnp.where((k0 + row_iota) < k_lim, acc * scale, -jnp.inf)
        return carry

    lax.fori_loop(0, nch, score_chunk, 0)

    ncc = (nch * ch + cc - 1) // cc

    def neg_chunk(c, carry):
        k0 = pl.multiple_of(c * ch, ch)
        key_ref[pl.ds(k0, ch), :] = jnp.full((ch, tq), -jnp.inf, F32)
        return carry

    lax.fori_loop(nch, ncc * (cc // ch), neg_chunk, 0)

    def count(pred):
        def body(c, cnt):
            k0 = pl.multiple_of(c * cc, cc)
            blk = key_ref[pl.ds(k0, cc), :]
            ones = jnp.where(pred(blk, k0), 1.0, 0.0)
            return cnt + jnp.sum(ones.reshape(cc // COUNT_ROWS, COUNT_ROWS, tq), axis=0)
        cnt = lax.fori_loop(0, ncc, body, jnp.zeros((COUNT_ROWS, tq), F32))
        return jnp.sum(cnt, axis=0, keepdims=True)

    def as_score(key):
        return lax.bitcast_convert_type(jnp.where(key < 0, key ^ 0x7FFFFFFF, key), F32)

    kf = float(ktop)
    n_adm = jnp.broadcast_to(jnp.asarray(k_lim, F32), (1, tq))
    all_adm = n_adm <= kf
    c0 = count(lambda blk, k0: blk >= 0.0)
    cand0 = jnp.where(c0 >= kf, 0, INT_MIN).astype(I32)
    cnt0 = jnp.where(c0 >= kf, c0, kf + 1.0)

    def settled(cnt):
        return jnp.min(jnp.where(jnp.logical_or(all_adm, cnt == kf), 1.0, 0.0)) > 0.0

    def bit_cond(st):
        bit, _, cnt = st
        return jnp.logical_and(bit >= 0, jnp.logical_not(settled(cnt)))

    def bit_body(st):
        bit, cand, cnt = st
        trial = cand | lax.shift_left(jnp.int32(1), bit)
        trial_f = as_score(trial)
        c = count(lambda blk, k0: blk >= trial_f)
        take = c >= kf
        return bit - 1, jnp.where(take, trial, cand), jnp.where(take, c, cnt)

    _, cand, cnt = lax.while_loop(bit_cond, bit_body, (jnp.int32(30), cand0, cnt0))
    thr = jnp.where(all_adm, -jnp.inf, as_score(cand))
    pcut0 = jnp.where(all_adm, 0, lpad).astype(I32)
    pcut_ref[...] = pcut0

    @pl.when(jnp.logical_not(settled(cnt)))
    def _():
        tie = jnp.logical_not(jnp.logical_or(all_adm, cnt == kf))
        need = kf - count(lambda blk, k0: blk > thr)
        nbits = max(1, (lpad - 1).bit_length())

        def pbody(it, q):
            trial = q | lax.shift_left(jnp.int32(1), nbits - 1 - it)
            c = count(lambda blk, k0: jnp.logical_and(blk == thr, (k0 + row_iota_cc) < trial))
            return jnp.where(c < need, trial, q)

        q = lax.fori_loop(0, nbits, pbody, jnp.zeros((1, tq), I32))
        pcut_ref[...] = jnp.where(tie, q + 1, pcut0)

    pcut = pcut_ref[...]

    def write_chunk(c, carry):
        k0 = pl.multiple_of(c * ch, ch)
        blk = key_ref[pl.ds(k0, ch), :]
        sel = jnp.logical_or(blk > thr, jnp.logical_and(blk == thr, (k0 + row_iota) < pcut))
        bias_ref[pl.ds(k0, ch), :] = jnp.where(sel, 0.0, MASK_NEG).astype(bias_ref.dtype)
        return carry

    lax.fori_loop(0, nch, write_chunk, 0)

    def fill_chunk(c, carry):
        k0 = pl.multiple_of(c * ch, ch)
        bias_ref[pl.ds(k0, ch), :] = jnp.full((ch, tq), MASK_NEG, bias_ref.dtype)
        return carry

    lax.fori_loop(nch, nch_all, fill_chunk, 0)


def _select(ki, qit, wit, *, l_valid, causal, ktop, idx_dim, t_len=None):
    nb, lpad, _ = ki.shape
    t_len = qit.shape[2] if t_len is None else t_len
    tq = _pick(t_len, 256)
    ch = _pick(lpad, 512)
    cc = next(c for c in (4 * ch, 3 * ch, 2 * ch, ch) if lpad % c == 0)
    assert ch >= ktop and lpad % ch == 0 and ch % COUNT_ROWS == 0
    kern = functools.partial(_sel_kernel, tq=tq, ch=ch, cc=cc, lpad=lpad, l_valid=l_valid,
                             causal=causal, ktop=ktop, idx_dim=idx_dim)
    return pl.pallas_call(
        kern,
        grid=(nb, t_len // tq),
        in_specs=[pl.BlockSpec((None, lpad, idx_dim), lambda b, i: (b, 0, 0)),
                  pl.BlockSpec((None, qit.shape[1], tq), lambda b, i: (b, 0, i)),
                  pl.BlockSpec((None, wit.shape[1], tq), lambda b, i: (b, 0, i))],
        out_specs=pl.BlockSpec((None, lpad, tq), lambda b, i: (b, 0, i)),
        out_shape=jax.ShapeDtypeStruct((nb, lpad, t_len), BF16),
        scratch_shapes=[pltpu.VMEM((lpad, tq), F32), pltpu.VMEM((1, tq), I32)],
        compiler_params=_cparams(("parallel", "arbitrary")),
        name="index_select",
    )(ki, qit, wit)


def _att_kernel(it_ref, jt_ref, fin_ref, q_ref, k_ref, vt_ref, b_ref, o_ref, m_ref, l_ref, acc_ref, s_ref, p_ref, *,
                tq, tk, n_kv, hd, rep):
    step = pl.program_id(1)

    @pl.when(jt_ref[step] == 0)
    def _():
        m_ref[...] = jnp.full(m_ref.shape, MASK_NEG, F32)
        l_ref[...] = jnp.zeros(l_ref.shape, F32)
        acc_ref[...] = jnp.zeros(acc_ref.shape, F32)

    def compute():
        bias = b_ref[...].astype(F32)
        bias = jnp.concatenate([bias] * rep, axis=1)
        r = rep * tq
        nblk = tk // ATT_ROWS
        def scores(g):
            k = k_ref[:, g * hd:(g + 1) * hd]
            s_ref[g] = jnp.dot(k, q_ref[g], preferred_element_type=F32) + bias

        def softmax_update(g):
            def max_body(b, m8):
                blk = s_ref[g, pl.ds(pl.multiple_of(b * ATT_ROWS, ATT_ROWS), ATT_ROWS), :]
                return jnp.maximum(m8, jnp.max(blk.reshape(ATT_ROWS // 8, 8, r), axis=0))

            m8 = lax.fori_loop(0, nblk, max_body, jnp.full((8, r), MASK_NEG, F32), unroll=True)
            m_prev = m_ref[g]
            m_new = jnp.maximum(m_prev, jnp.max(m8, axis=0, keepdims=True))
            alpha = jnp.exp2(m_prev - m_new)

            def exp_body(b, l8):
                r0 = pl.multiple_of(b * ATT_ROWS, ATT_ROWS)
                p = jnp.exp2(s_ref[g, pl.ds(r0, ATT_ROWS), :] - m_new)
                p_ref[g, pl.ds(r0, ATT_ROWS), :] = p.astype(BF16)
                return l8 + jnp.sum(p.reshape(ATT_ROWS // 8, 8, r), axis=0)

            l8 = lax.fori_loop(0, nblk, exp_body, jnp.zeros((8, r), F32), unroll=True)
            l_ref[g] = alpha * l_ref[g] + jnp.sum(l8, axis=0, keepdims=True)
            m_ref[g] = m_new
            return alpha

        def weighted_values(g, alpha):
            pv = jnp.dot(vt_ref[g * hd:(g + 1) * hd, :], p_ref[g], preferred_element_type=F32)
            acc_ref[g] = alpha * acc_ref[g] + pv

        scores(0)
        for g in range(n_kv):
            if g + 1 < n_kv:
                scores(g + 1)
            weighted_values(g, softmax_update(g))

    compute()

    @pl.when(fin_ref[step] == 1)
    def _():
        for g in range(n_kv):
            o_t = (acc_ref[g] / l_ref[g]).T
            for rr in range(rep):
                h = g * rep + rr
                o_ref[:, h * hd:(h + 1) * hd] = o_t[rr * tq:(rr + 1) * tq, :].astype(o_ref.dtype)


def _attention(qt, k, vt, bias_t, *, tq, causal, nt=None):
    nb, nt_all, n_kv, hd, r = qt.shape
    nt = nt_all if nt is None else nt
    rep = r // tq
    lpad = k.shape[1]
    tk = _pick(lpad, 512)
    nk = lpad // tk
    n_keys = [min(nk, ((i + 1) * tq - 1) // tk + 1) if causal else nk for i in range(nt)]
    it = jnp.asarray([i for i in range(nt) for _ in range(n_keys[i])], I32)
    jt = jnp.asarray([j for i in range(nt) for j in range(n_keys[i])], I32)
    fin = jnp.asarray([int(j == n_keys[i] - 1) for i in range(nt) for j in range(n_keys[i])], I32)
    kern = functools.partial(_att_kernel, tq=tq, tk=tk, n_kv=n_kv, hd=hd, rep=rep)
    grid_spec = pltpu.PrefetchScalarGridSpec(
        num_scalar_prefetch=3,
        grid=(nb, int(it.shape[0])),
        in_specs=[pl.BlockSpec((None, None, n_kv, hd, r), lambda b, s, it, jt, fin: (b, it[s], 0, 0, 0)),
                  pl.BlockSpec((None, tk, n_kv * hd), lambda b, s, it, jt, fin: (b, jt[s], 0)),
                  pl.BlockSpec((None, n_kv * hd, tk), lambda b, s, it, jt, fin: (b, 0, jt[s])),
                  pl.BlockSpec((None, tk, tq), lambda b, s, it, jt, fin: (b, jt[s], it[s]))],
        out_specs=pl.BlockSpec((tq, n_kv * rep * hd), lambda b, s, it, jt, fin: (b * nt + it[s], 0)),
        scratch_shapes=[pltpu.VMEM((n_kv, 1, r), F32),
                        pltpu.VMEM((n_kv, 1, r), F32),
                        pltpu.VMEM((n_kv, hd, r), F32),
                        pltpu.VMEM((n_kv, tk, r), F32),
                        pltpu.VMEM((n_kv, tk, r), BF16)],
    )
    return pl.pallas_call(
        kern,
        grid_spec=grid_spec,
        out_shape=jax.ShapeDtypeStruct((nb * nt * tq, n_kv * rep * hd), BF16),
        compiler_params=_cparams(("parallel", "arbitrary")),
        name="sparse_attention",
    )(it, jt, fin, qt, k, vt, bias_t)


def _mematt_kernel(q_ref, k_ref, v_ref, o_ref, *, n_heads, hd):
    scale = hd ** -0.5
    for h in range(n_heads):
        q = q_ref[:, h * hd:(h + 1) * hd]
        k = k_ref[:, h * hd:(h + 1) * hd].astype(BF16)
        v = v_ref[:, h * hd:(h + 1) * hd].astype(BF16)
        s = lax.dot_general(q, k, (((1,), (1,)), ((), ())), preferred_element_type=F32) * scale
        m = jnp.max(s, axis=-1, keepdims=True)
        p = jnp.exp(s - m)
        l = jnp.sum(p, axis=-1, keepdims=True)
        o = jnp.dot(p.astype(BF16), v, preferred_element_type=F32) / l
        o_ref[:, h * hd:(h + 1) * hd] = o.astype(o_ref.dtype)


def _mem_attention(mq, mk, mv, *, row_off, nb, t_len, n_heads, hd):
    tq = _pick(t_len, 512)
    nt = t_len // tq
    off = row_off // tq
    n_mem = mk.shape[1]
    w = n_heads * hd
    return pl.pallas_call(
        functools.partial(_mematt_kernel, n_heads=n_heads, hd=hd),
        grid=(nb, nt),
        in_specs=[pl.BlockSpec((tq, w), lambda b, i: (off + b * nt + i, 0)),
                  pl.BlockSpec((None, n_mem, w), lambda b, i: (b, 0, 0)),
                  pl.BlockSpec((None, n_mem, w), lambda b, i: (b, 0, 0))],
        out_specs=pl.BlockSpec((tq, w), lambda b, i: (b * nt + i, 0)),
        out_shape=jax.ShapeDtypeStruct((nb * t_len, w), BF16),
        compiler_params=_cparams(("parallel", "parallel")),
        name="memory_attention",
    )(mq, mk, mv)


def _router_kernel(x_ref, g_ref, w_ref, b_ref, xn_ref, idx_ref, p_ref):
    x = x_ref[...]
    ms = jnp.mean(x * x, axis=-1, keepdims=True)
    xn = x * lax.rsqrt(ms + EPS) * g_ref[...]
    xn_ref[...] = xn
    logits = jnp.dot(xn.astype(BF16), w_ref[...], preferred_element_type=F32) + b_ref[...]
    lane = lax.broadcasted_iota(I32, logits.shape, 1)
    lane_f = lane.astype(F32)
    vals, idxs = [], []
    l = logits
    for _ in range(TOP_K):
        m = jnp.max(l, axis=-1, keepdims=True)
        ix = jnp.min(jnp.where(l == m, lane_f, float(LANES)), axis=-1, keepdims=True)
        vals.append(m)
        idxs.append(ix)
        l = jnp.where(lane_f == ix, -jnp.inf, l)
    es = [jnp.exp(v - vals[0]) for v in vals]
    den = es[0]
    for e in es[1:]:
        den = den + e
    idx_out = jnp.zeros(logits.shape, F32)
    p_out = jnp.zeros(logits.shape, F32)
    for r in range(TOP_K):
        idx_out = jnp.where(lane == r, idxs[r], idx_out)
        p_out = jnp.where(lane == r, es[r] / den, p_out)
    idx_ref[...] = idx_out.astype(I32)
    p_ref[...] = p_out


def _router(x1, g, w_router, b_router):
    n, d = x1.shape
    ne = w_router.shape[1]
    tm = _pick(n, 512)
    w_pad = jnp.zeros((d, LANES), BF16).at[:, :ne].set(w_router.astype(BF16))
    b_pad = jnp.full((1, LANES), MASK_NEG, F32).at[0, :ne].set(b_router)
    return pl.pallas_call(
        _router_kernel,
        grid=(n // tm,),
        in_specs=[pl.BlockSpec((tm, d), lambda i: (i, 0)),
                  pl.BlockSpec((1, d), lambda i: (0, 0)),
                  pl.BlockSpec((d, LANES), lambda i: (0, 0)),
                  pl.BlockSpec((1, LANES), lambda i: (0, 0))],
        out_specs=[pl.BlockSpec((tm, d), lambda i: (i, 0)),
                   pl.BlockSpec((tm, LANES), lambda i: (i, 0)),
                   pl.BlockSpec((tm, LANES), lambda i: (i, 0))],
        out_shape=[jax.ShapeDtypeStruct((n, d), F32),
                   jax.ShapeDtypeStruct((n, LANES), I32),
                   jax.ShapeDtypeStruct((n, LANES), F32)],
        compiler_params=_cparams(("parallel",)),
        name="router",
    )(x1, g.reshape(1, d), w_pad, b_pad)


def _row_copy(src, dst, src_row, dst_row, sem):
    return pltpu.make_async_copy(src.at[pl.ds(src_row, 1)], dst.at[pl.ds(dst_row, 1)], sem)


def _for_rows(n_rows, row_fn):
    n_blocks = n_rows // ROW_UNROLL

    def block(b, c):
        for u in range(ROW_UNROLL):
            row_fn(b * ROW_UNROLL + u)
        return c

    lax.fori_loop(0, n_blocks, block, 0)

    def tail(r, c):
        row_fn(r)
        return c

    lax.fori_loop(n_blocks * ROW_UNROLL, n_rows, tail, 0)


def _moe_kernel(te_ref, tp_ref, tr_ref, nu_ref, ord_ref, x_hbm, wg_ref, wu_ref, bg_ref, bu_ref, wd_ref, bd_ref,
                y_hbm, xf_ref, xb_ref, acc_ref, gsem, ssem, *, tm, nf):
    t = pl.program_id(0)
    f = pl.program_id(1)
    shift = TOP_K.bit_length() - 1
    n_tok = x_hbm.shape[0]

    def scatter(tile, op):
        base = tp_ref[tile]

        def row(r):
            a = ord_ref[base + r]
            dst = (a & (TOP_K - 1)) * n_tok + lax.shift_right_logical(a, shift)
            cp = _row_copy(acc_ref, y_hbm, r, dst, ssem)
            cp.start() if op == "start" else cp.wait()

        _for_rows(tr_ref[tile], row)

    def gather(op):
        base = tp_ref[t]

        def row(r):
            cp = _row_copy(x_hbm, xf_ref, lax.shift_right_logical(ord_ref[base + r], shift), r, gsem)
            cp.start() if op == "start" else cp.wait()

        _for_rows(tm, row)

    @pl.when(t < nu_ref[0])
    def _():
        @pl.when(f == 0)
        def _():
            gather("start")
            gather("wait")
            xb_ref[...] = xf_ref[...].astype(BF16)

        xb = xb_ref[...]
        hg = jnp.dot(xb, wg_ref[...].astype(BF16), preferred_element_type=F32) + bg_ref[...]
        hu = jnp.dot(xb, wu_ref[...].astype(BF16), preferred_element_type=F32) + bu_ref[...]
        gate = jnp.minimum(hg, SWIGLU_LIMIT)
        up = jnp.clip(hu, -SWIGLU_LIMIT, SWIGLU_LIMIT)
        act = (up + 1.0) * gate * jax.nn.sigmoid(SWIGLU_ALPHA * gate)
        contrib = jnp.dot(act.astype(BF16), wd_ref[...].astype(BF16), preferred_element_type=F32)

        @pl.when(f == 0)
        def _():
            @pl.when(t > 0)
            def _():
                scatter(t - 1, "wait")

            acc_ref[...] = contrib + bd_ref[...]

        @pl.when(f > 0)
        def _():
            acc_ref[...] += contrib

        @pl.when(f == nf - 1)
        def _():
            scatter(t, "start")

            @pl.when(t == nu_ref[0] - 1)
            def _():
                scatter(t, "wait")


def _moe_experts(xn, tile_expert, tile_pos, tile_rows, n_used, order, w_up_gate, b_up_gate, w_down, b_down, *, tm, fc):
    n, d = xn.shape
    ne, _, ff2 = w_up_gate.shape
    ff = ff2 // 2
    nf = ff // fc
    mt = tile_expert.shape[0]

    def fe(t, f, nu):
        return jnp.where(t < nu[0], f, nf - 1)

    grid_spec = pltpu.PrefetchScalarGridSpec(
        num_scalar_prefetch=5,
        grid=(mt, nf),
        in_specs=[
            pl.BlockSpec(memory_space=pl.ANY),
            pl.BlockSpec((None, d, fc), lambda t, f, te, tp, tr, nu, o: (te[t], 0, fe(t, f, nu))),
            pl.BlockSpec((None, d, fc), lambda t, f, te, tp, tr, nu, o: (te[t], 0, nf + fe(t, f, nu))),
            pl.BlockSpec((None, 1, fc), lambda t, f, te, tp, tr, nu, o: (te[t], 0, fe(t, f, nu))),
            pl.BlockSpec((None, 1, fc), lambda t, f, te, tp, tr, nu, o: (te[t], 0, nf + fe(t, f, nu))),
            pl.BlockSpec((None, fc, d), lambda t, f, te, tp, tr, nu, o: (te[t], fe(t, f, nu), 0)),
            pl.BlockSpec((None, 1, d), lambda t, f, te, tp, tr, nu, o: (te[t], 0, 0)),
        ],
        out_specs=pl.BlockSpec(memory_space=pl.ANY),
        scratch_shapes=[pltpu.VMEM((tm, d), F32), pltpu.VMEM((tm, d), BF16), pltpu.VMEM((tm, d), F32),
                        pltpu.SemaphoreType.DMA(()), pltpu.SemaphoreType.DMA(())],
    )
    return pl.pallas_call(
        functools.partial(_moe_kernel, tm=tm, nf=nf),
        grid_spec=grid_spec,
        out_shape=jax.ShapeDtypeStruct((n * TOP_K, d), F32),
        compiler_params=_cparams(("arbitrary", "arbitrary")),
        name="moe_experts",
    )(tile_expert, tile_pos, tile_rows, n_used, order, xn, w_up_gate, w_up_gate,
      b_up_gate.reshape(ne, 1, ff2), b_up_gate.reshape(ne, 1, ff2), w_down, b_down.reshape(ne, 1, d))


def _combine_kernel(x_ref, p_ref, g_ref, *refs):
    y_refs, o_ref = refs[:TOP_K], refs[TOP_K]
    p = p_ref[...]
    x = x_ref[...]
    for k in range(TOP_K):
        x = x + p[:, k:k + 1] * y_refs[k][...]
    ms = jnp.mean(x * x, axis=-1, keepdims=True)
    o_ref[...] = x * lax.rsqrt(ms + EPS) * g_ref[...]


def _combine(x1, probs, y, g_final):
    n, d = x1.shape
    tm = _pick(n, 256)
    nt = n // tm
    y_specs = [pl.BlockSpec((tm, d), functools.partial(lambda i, k: (k * nt + i, 0), k=k)) for k in range(TOP_K)]
    return pl.pallas_call(
        _combine_kernel,
        grid=(nt,),
        in_specs=[pl.BlockSpec((tm, d), lambda i: (i, 0)),
                  pl.BlockSpec((tm, LANES), lambda i: (i, 0)),
                  pl.BlockSpec((1, d), lambda i: (0, 0))] + y_specs,
        out_specs=pl.BlockSpec((tm, d), lambda i: (i, 0)),
        out_shape=jax.ShapeDtypeStruct((n, d), F32),
        compiler_params=_cparams(("parallel",)),
        name="moe_combine",
    )(x1, probs, g_final.reshape(1, d), *([y] * TOP_K))


def _rope_tables(pos, dim):
    half = dim // 2
    inv_freq = ROPE_THETA ** (-jnp.arange(half, dtype=F32) / half)
    ang = pos.astype(F32)[:, None] * inv_freq[None, :]
    cos, sin = jnp.cos(ang), jnp.sin(ang)
    reps = LANES // dim
    cos_t = jnp.tile(jnp.concatenate([cos, cos], axis=-1), (1, reps))
    sin_t = jnp.tile(jnp.concatenate([-sin, sin], axis=-1), (1, reps))
    return cos_t, sin_t


def _pad_len(l, mult):
    return (l + mult - 1) // mult * mult


def kernel(x_prompt, x_sample, cache_attn_k, cache_attn_v, cache_idx_k, state_conv, cache_mem_k, cache_mem_v, mem_prompt, g_mix, w_comb, conv_w, w_conv_proj, w_attn_proj, g_mem, w_mem_kv, w_mem_proj, w_merge_out, g_moe, w_router, b_router, w_up_gate, b_up_gate, w_down, b_down, g_final):
    bp, t_p, d = x_prompt.shape
    bs, t_s, _ = x_sample.shape
    assert bp == 1 and g_mix.shape[0] == 1
    past = cache_attn_k.shape[2]
    n_kv, hd = cache_attn_k.shape[3], cache_attn_k.shape[4]
    idx_dim = cache_idx_k.shape[3]
    cw = conv_w.shape[2]
    attn_q = w_attn_proj.shape[1]
    n_heads = attn_q // hd
    attn_kv = n_kv * hd
    idx_q = IDX_HEADS * idx_dim
    n_mem, mem_heads, mem_hd = cache_mem_k.shape[2], cache_mem_k.shape[3], cache_mem_k.shape[4]
    mem_q = mem_heads * mem_hd
    n_p, n_s = bp * t_p, bs * t_s
    n = n_p + n_s

    o_cin, o_cb, o_cc = 0, cw, 2 * cw
    o_q = 3 * cw
    o_k = o_q + attn_q
    o_v = o_k + attn_kv
    o_qi = o_v + attn_kv
    o_ki = o_qi + idx_q
    o_wi = o_ki + idx_dim
    o_mq = o_wi + IDX_HEADS
    o_g = o_mq + mem_q
    wc = w_comb[0]
    w_main = wc[:, :o_ki].astype(BF16)
    w_kw = jnp.zeros((d, LANES), BF16).at[:, :idx_dim + IDX_HEADS].set(wc[:, o_ki:o_mq].astype(BF16))
    w_mq = wc[:, o_mq:o_g].astype(BF16)
    w_gates = wc[:, o_g:].astype(BF16)

    x = jnp.concatenate([x_prompt.reshape(n_p, d), x_sample.reshape(n_s, d)], axis=0)
    pos = jnp.concatenate([jnp.arange(t_p, dtype=I32),
                           jnp.tile(past + jnp.arange(t_s, dtype=I32), bs)])
    cos_h, sin_h = _rope_tables(pos, hd)
    cos_i, sin_i = _rope_tables(pos, idx_dim)
    tm = _pick(n, 512)
    rowtab = lambda a: (a, (tm, LANES), lambda i, j: (i, 0))

    xn = _rmsnorm(x, g_mix[0], BF16)

    tn_c = _pick(cw, 512)
    nbc = cw // tn_c

    def ep_conv(accs, extras, outs):
        outs[0][...] = accs[2] * accs[0]
        outs[1][...] = accs[1]

    u, cb = _mm(xn, [(w_main, 0), (w_main, nbc), (w_main, 2 * nbc)], ep_conv,
                [(cw, F32, tn_c, None), (cw, F32, tn_c, None)], tn=tn_c, name="proj_conv")

    attn_scale = hd ** -0.5 * LOG2E
    tn_q = _pick(attn_q, 1024)

    rep = n_heads // n_kv
    tq_p = _pick(t_p, 256)
    fused_t = tm % tq_p == 0 and n_p % tm == 0 and t_p % LANES == 0 and tn_q % (rep * hd) == 0
    qpt = tm // tq_p
    gpc = tn_q // (rep * hd)

    def ep_q(accs, extras, outs):
        pieces = _rope_groups(accs[0], extras[0][...], extras[1][...], hd // 2)
        for c, p in enumerate(pieces):
            p = p * attn_scale
            outs[0][:, c * LANES:(c + 1) * LANES] = p.astype(BF16)
            if fused_t:
                p_t = p.T
                g_loc, rr = divmod(c, rep)
                for t in range(qpt):
                    outs[1][t, g_loc, :, rr * tq_p:(rr + 1) * tq_p] = p_t[:, t * tq_p:(t + 1) * tq_p].astype(BF16)

    q_outs = [(attn_q, BF16, tn_q, None)]
    if fused_t:
        q_outs.append(((n // tq_p, n_kv, hd, rep * tq_p), BF16, (qpt, gpc, hd, rep * tq_p),
                       lambda i, j: (i, j, 0, 0), None))
    q_res = _mm(xn, [(w_main, o_q // tn_q)], ep_q, q_outs,
                extras=[rowtab(cos_h), rowtab(sin_h)], tn=tn_q, name="proj_q")
    q_r = q_res[0]

    def ep_k(accs, extras, outs):
        pieces = _rope_groups(accs[0], extras[0][...], extras[1][...], hd // 2)
        for c, p in enumerate(pieces):
            outs[0][:, c * LANES:(c + 1) * LANES] = p
            outs[1][:, c * LANES:(c + 1) * LANES] = p.astype(BF16)

    k_f, k_b = _mm(xn, [(w_main, o_k // attn_kv)], ep_k,
                   [(attn_kv, F32, attn_kv, None), (attn_kv, BF16, attn_kv, None)],
                   extras=[rowtab(cos_h), rowtab(sin_h)], tn=attn_kv, name="proj_k")

    def ep_v(accs, extras, outs):
        outs[0][...] = accs[0]
        outs[1][...] = accs[0].astype(BF16)
        if fused_t:
            outs[2][...] = accs[0].T.astype(BF16)

    v_outs = [(attn_kv, F32, attn_kv, None), (attn_kv, BF16, attn_kv, None)]
    if fused_t:
        v_outs.append(((attn_kv, n), BF16, (attn_kv, tm), lambda i, j: (0, i), None))
    v_res = _mm(xn, [(w_main, o_v // attn_kv)], ep_v, v_outs, tn=attn_kv, name="proj_v")
    v_f, v_b = v_res[0], v_res[1]

    tn_i = _pick(idx_q, 1024)

    fused_i = fused_t and tn_i == idx_q

    def ep_qi(accs, extras, outs):
        pieces = _rope_groups(accs[0], extras[0][...], extras[1][...], idx_dim // 2)
        for c, p in enumerate(pieces):
            outs[0][:, c * LANES:(c + 1) * LANES] = p.astype(BF16)
            if fused_i:
                outs[1][c * LANES:(c + 1) * LANES, :] = p.T.astype(BF16)

    qi_outs = [(idx_q, BF16, tn_i, None)]
    if fused_i:
        qi_outs.append(((idx_q, n), BF16, (idx_q, tm), lambda i, j: (0, i), None))
    qi_res = _mm(xn, [(w_main, o_qi // tn_i)], ep_qi, qi_outs,
                 extras=[rowtab(cos_i), rowtab(sin_i)], tn=tn_i, name="proj_qi")
    qi_r = qi_res[0]

    wi_scale = IDX_HEADS ** -0.5

    def ep_kw(accs, extras, outs):
        a = accs[0]
        (roped,) = _rope_groups(a, extras[0][...], extras[1][...], idx_dim // 2)
        lane = lax.broadcasted_iota(I32, a.shape, 1)
        outs[0][...] = jnp.where(lane < idx_dim, roped, a * wi_scale)

    (kw,) = _mm(xn, [(w_kw, 0)], ep_kw, [(LANES, F32, LANES, None)],
                extras=[rowtab(cos_i), rowtab(sin_i)], tn=LANES, name="proj_ki_wi")
    ki_f = kw[:, :idx_dim]
    wi = kw[:, idx_dim:idx_dim + IDX_HEADS]

    def ep_cast(accs, extras, outs):
        outs[0][...] = accs[0].astype(outs[0].dtype)

    tn_m = _pick(mem_q, 1024)
    (mq,) = _mm(xn, [(w_mq, 0)], ep_cast, [(mem_q, BF16, tn_m, None)], tn=tn_m, name="proj_mq")

    def ep_sig(accs, extras, outs):
        outs[0][...] = jax.nn.sigmoid(accs[0])

    tn_g = _pick(d, 1024)
    (sg,) = _mm(xn, [(w_gates, 0)], ep_sig, [(3 * d, F32, tn_g, None)], tn=tn_g, name="proj_gates")

    conv_p, hist_p = _short_conv(u, cb, jnp.zeros((bp, 2, cw), F32), conv_w[0], 0, bp, t_p)
    conv_s, hist_s = _short_conv(u, cb, state_conv[0], conv_w[0], n_p, bs, t_s)
    conv_pre = jnp.concatenate([conv_p, conv_s], axis=0)

    ki_b = ki_f.astype(BF16)

    def to_qt(q, nb, t_len, tq):
        q = q.reshape(nb, t_len // tq, tq, n_kv, rep, hd)
        return jnp.transpose(q, (0, 1, 3, 5, 4, 2)).reshape(nb, t_len // tq, n_kv, hd, rep * tq)

    l_p = t_p
    lpad_p = _pad_len(l_p, LANES)
    pad_p = lpad_p - l_p
    qit_p = qi_res[1][None] if fused_i else qi_r[:n_p].T[None]
    bias_p = _select(jnp.pad(ki_b[:n_p], ((0, pad_p), (0, 0)))[None], qit_p, wi.T[None], t_len=t_p,
                     l_valid=l_p, causal=True, ktop=min(TOPK_MAX, l_p // 4), idx_dim=idx_dim)
    kp = jnp.pad(k_b[:n_p], ((0, pad_p), (0, 0)))[None]
    if fused_t:
        qt_p, vtp = q_res[1][None], v_res[2][None]
    else:
        qt_p, vtp = to_qt(q_r[:n_p], bp, t_p, tq_p), jnp.pad(v_b[:n_p].T, ((0, 0), (0, pad_p)))[None]
    attn_p = _attention(qt_p, kp, vtp, bias_p, tq=tq_p, causal=True, nt=t_p // tq_p)

    l_s = past + t_s
    lpad_s = _pad_len(l_s, 512)
    pad_s = lpad_s - l_s
    ki_all = jnp.concatenate([cache_idx_k[0].astype(BF16), ki_b[n_p:].reshape(bs, t_s, idx_dim)], axis=1)
    bias_s = _select(jnp.pad(ki_all, ((0, 0), (0, pad_s), (0, 0))),
                     jnp.swapaxes(qi_r[n_p:].reshape(bs, t_s, idx_q), 1, 2),
                     jnp.swapaxes(wi[n_p:].reshape(bs, t_s, IDX_HEADS), 1, 2),
                     l_valid=l_s, causal=False, ktop=min(TOPK_MAX, l_s // 4), idx_dim=idx_dim)
    ks = jnp.concatenate([cache_attn_k[0].reshape(bs, past, attn_kv).astype(BF16),
                          k_b[n_p:].reshape(bs, t_s, attn_kv)], axis=1)
    vs = jnp.concatenate([cache_attn_v[0].reshape(bs, past, attn_kv).astype(BF16),
                          v_b[n_p:].reshape(bs, t_s, attn_kv)], axis=1)
    ks = jnp.pad(ks, ((0, 0), (0, pad_s), (0, 0)))
    vts = jnp.pad(jnp.swapaxes(vs, 1, 2), ((0, 0), (0, 0), (0, pad_s)))
    attn_s = _attention(to_qt(q_r[n_p:], bs, t_s, t_s), ks, vts, bias_s, tq=t_s, causal=False)
    attn = jnp.concatenate([attn_p, attn_s], axis=0)

    memn = _rmsnorm(mem_prompt.reshape(bp * n_mem, d), g_mem[0], BF16)
    tn_kv = _pick(2 * mem_q, 1024)

    def ep_f32(accs, extras, outs):
        outs[0][...] = accs[0]

    (mem_kv,) = _mm(memn, [(w_mem_kv[0].astype(BF16), 0)], ep_f32, [(2 * mem_q, F32, tn_kv, None)],
                    tn=tn_kv, name="proj_mem_kv")
    mk_p = mem_kv[:, :mem_q].reshape(bp, n_mem, mem_q)
    mv_p = mem_kv[:, mem_q:].reshape(bp, n_mem, mem_q)
    memo_p = _mem_attention(mq, mk_p, mv_p, row_off=0, nb=bp, t_len=t_p, n_heads=mem_heads, hd=mem_hd)
    memo_s = _mem_attention(mq, cache_mem_k[0].reshape(bs, n_mem, mem_q), cache_mem_v[0].reshape(bs, n_mem, mem_q),
                            row_off=n_p, nb=bs, t_len=t_s, n_heads=mem_heads, hd=mem_hd)
    memo = jnp.concatenate([memo_p, memo_s], axis=0)

    ngb = d // tn_g

    def gate_extra(which):
        return (sg, (tm, tn_g), functools.partial(lambda i, j, o: (i, o + j), o=which * ngb))

    def ep_gate0(accs, extras, outs):
        outs[0][...] = extras[0][...] * accs[0]

    def ep_gate(accs, extras, outs):
        outs[0][...] = (extras[1][...] + extras[0][...] * accs[0]).astype(outs[0].dtype)

    prev = lambda a: (a, (tm, tn_g), lambda i, j: (i, j))
    (m1,) = _mm(conv_pre, [(w_conv_proj[0].astype(BF16), 0)], ep_gate0, [(d, F32, tn_g, None)],
                extras=[gate_extra(0)], tn=tn_g, name="merge_conv")
    (m2,) = _mm(attn, [(w_attn_proj[0].astype(BF16), 0)], ep_gate, [(d, F32, tn_g, None)],
                extras=[gate_extra(1), prev(m1)], tn=tn_g, name="merge_attn")
    (m3,) = _mm(memo, [(w_mem_proj[0].astype(BF16), 0)], ep_gate, [(d, BF16, tn_g, None)],
                extras=[gate_extra(2), prev(m2)], tn=tn_g, name="merge_mem")

    def ep_res(accs, extras, outs):
        outs[0][...] = extras[0][...] + accs[0]

    (x1,) = _mm(m3, [(w_merge_out[0].astype(BF16), 0)], ep_res, [(d, F32, tn_g, None)],
                extras=[prev(x)], tn=tn_g, name="merge_out")

    xn2, ridx, rp = _router(x1, g_moe[0], w_router[0], b_router[0])
    ne = w_router.shape[2]
    ff = w_down.shape[2]
    nk = n * TOP_K
    tme = MOE_TILE if nk >= 16 * MOE_TILE else 128
    e_flat = ridx[:, :TOP_K].reshape(nk)
    order = jnp.argsort(e_flat, stable=True).astype(I32)
    experts = jnp.arange(ne, dtype=I32)
    counts = jnp.sum((e_flat[:, None] == experts[None, :]).astype(I32), axis=0)
    tiles_per = (counts + tme - 1) // tme
    tile_end = jnp.cumsum(tiles_per)
    tile_first = tile_end - tiles_per
    cstart = jnp.cumsum(counts) - counts
    n_used = tile_end[-1].astype(I32)
    mt = nk // tme + ne
    tile_ids = jnp.arange(mt, dtype=I32)
    owner = jnp.logical_and(tile_ids[:, None] >= tile_first[None, :], tile_ids[:, None] < tile_end[None, :])
    pick = lambda v: jnp.sum(jnp.where(owner, v[None, :], 0), axis=1).astype(I32)
    k_in = tile_ids - pick(tile_first)
    tile_pos = pick(cstart) + k_in * tme
    tile_rows = jnp.clip(pick(counts) - k_in * tme, 0, tme)
    last_e = jnp.sum(jnp.where(tile_ids == n_used - 1, pick(experts), 0))
    tile_expert = jnp.where(tile_ids < n_used, pick(experts), last_e).astype(I32)
    order_pad = jnp.concatenate([order, jnp.zeros((tme,), I32)])
    y_assign = _moe_experts(xn2, tile_expert, tile_pos, tile_rows, n_used.reshape(1), order_pad,
                            w_up_gate[0], b_up_gate[0], w_down[0], b_down[0], tm=tme, fc=_pick(ff, 256))
    y = _combine(x1, rp, y_assign, g_final)

    y_prompt = y[:n_p].reshape(bp, t_p, d)
    y_sample = y[n_p:].reshape(bs, t_s, d)
    new_k_p = k_f[:n_p].reshape(1, bp, t_p, n_kv, hd)
    new_v_p = v_f[:n_p].reshape(1, bp, t_p, n_kv, hd)
    new_ki_p = ki_f[:n_p].reshape(1, bp, t_p, idx_dim)
    new_k_s = k_f[n_p:].reshape(1, bs, t_s, n_kv, hd)
    new_v_s = v_f[n_p:].reshape(1, bs, t_s, n_kv, hd)
    new_ki_s = ki_f[n_p:].reshape(1, bs, t_s, idx_dim)
    return (y_prompt, y_sample, new_k_p, new_v_p, new_ki_p, hist_p[None],
            mk_p.reshape(1, bp, n_mem, mem_heads, mem_hd), mv_p.reshape(1, bp, n_mem, mem_heads, mem_hd),
            new_k_s, new_v_s, new_ki_s, hist_s[None])
```

```python
import functools

import jax
import jax.numpy as jnp
from jax import lax
from jax.experimental import pallas as pl
from jax.experimental.pallas import tpu as pltpu

F32 = jnp.float32
BF16 = jnp.bfloat16
I32 = jnp.int32

CHUNK = 64
IDX_HEADS = 16
TOPK_MAX = 256
TOP_K = 4
SWIGLU_LIMIT = 7.0
SWIGLU_ALPHA = 1.702
ROPE_THETA = 10000.0
EPS = 1e-6

LANES = 128
VMEM_LIMIT = 56 * 1024 * 1024
LOG2E = 1.4426950408889634
DEN_ROWS = 16
ROW_UNROLL = 8
ATT_ROWS = 32
COUNT_ROWS = 64
MOE_TILE = 1024
MASK_NEG = -1e30
INT_MIN = -2147483648
NEGINF_KEY = -2139095041


def _cparams(sem):
    return pltpu.CompilerParams(dimension_semantics=sem, vmem_limit_bytes=VMEM_LIMIT)


def _pick(n, pref):
    if n <= pref:
        return n
    t = pref
    while n % t:
        t //= 2
    return t


def _rms_kernel(x_ref, g_ref, o_ref):
    x = x_ref[...]
    ms = jnp.mean(x * x, axis=-1, keepdims=True)
    o_ref[...] = (x * lax.rsqrt(ms + EPS) * g_ref[...]).astype(o_ref.dtype)


def _rmsnorm(x, g, out_dtype):
    n, d = x.shape
    tm = _pick(n, 512)
    return pl.pallas_call(
        _rms_kernel,
        grid=(n // tm,),
        in_specs=[pl.BlockSpec((tm, d), lambda i: (i, 0)),
                  pl.BlockSpec((1, d), lambda i: (0, 0))],
        out_specs=pl.BlockSpec((tm, d), lambda i: (i, 0)),
        out_shape=jax.ShapeDtypeStruct((n, d), out_dtype),
        compiler_params=_cparams(("parallel",)),
        name="rmsnorm",
    )(x, g.reshape(1, d))


def _mm_kernel(epilogue, n_w, n_extra, x_ref, *refs):
    w_refs = refs[:n_w]
    extras = refs[n_w:n_w + n_extra]
    outs = refs[n_w + n_extra:]
    x = x_ref[...]
    accs = [jnp.dot(x, w[...], preferred_element_type=F32) for w in w_refs]
    epilogue(accs, extras, outs)


def _mm(x, ws, epilogue, outs, extras=(), *, tn, tm=512, name="mm"):
    n, k = x.shape
    tm = _pick(n, tm)
    ncol = outs[0][0] // outs[0][2]
    in_specs = [pl.BlockSpec((tm, k), lambda i, j: (i, 0))]
    args = [x]
    for w, off in ws:
        in_specs.append(pl.BlockSpec((k, tn), functools.partial(lambda i, j, o: (0, o + j), o=off)))
        args.append(w)
    for a, bs, im in extras:
        in_specs.append(pl.BlockSpec(bs, im))
        args.append(a)
    out_specs, out_shape = [], []
    for o in outs:
        if len(o) == 5:
            shape, dt, bs, im, _ = o
            out_specs.append(pl.BlockSpec(bs, im))
            out_shape.append(jax.ShapeDtypeStruct(shape, dt))
            continue
        width, dt, bw, im = o
        out_specs.append(pl.BlockSpec((tm, bw), im if im is not None else (lambda i, j: (i, j))))
        out_shape.append(jax.ShapeDtypeStruct((n, width), dt))
    res = pl.pallas_call(
        functools.partial(_mm_kernel, epilogue, len(ws), len(extras)),
        grid=(n // tm, ncol),
        in_specs=in_specs,
        out_specs=out_specs,
        out_shape=out_shape,
        compiler_params=_cparams(("parallel", "arbitrary")),
        name=name,
    )(*args)
    return res


def _rope_groups(acc, cos, sin, half):
    tn = acc.shape[1]
    lane = lax.broadcasted_iota(I32, (acc.shape[0], LANES), 1)
    pieces = []
    for c in range(tn // LANES):
        a = acc[:, c * LANES:(c + 1) * LANES]
        if 2 * half == LANES:
            partner = pltpu.roll(a, half, axis=1)
        else:
            first = (lane & (2 * half - 1)) < half
            partner = jnp.where(first, pltpu.roll(a, LANES - half, axis=1), pltpu.roll(a, half, axis=1))
        pieces.append(a * cos + partner * sin)
    return pieces


def _conv_kernel(u_ref, cb_ref, h_ref, w_ref, o_ref, nh_ref, carry_ref):
    t = pl.program_id(1)

    @pl.when(t == 0)
    def _():
        carry_ref[0:2, :] = h_ref[...]

    u = u_ref[...]
    tm = u.shape[0]
    h0 = carry_ref[0:1, :]
    h1 = carry_ref[1:2, :]
    row = lax.broadcasted_iota(I32, u.shape, 0)
    p1 = jnp.where(row == 0, h1, pltpu.roll(u, 1, axis=0))
    p2 = jnp.where(row == 0, h0, jnp.where(row == 1, h1, pltpu.roll(u, 2, axis=0)))
    w = w_ref[...]
    y = w[0:1, :] * p2 + w[1:2, :] * p1 + w[2:3, :] * u
    o_ref[...] = (cb_ref[...] * y).astype(o_ref.dtype)
    last = u[tm - 2:tm, :]
    carry_ref[0:2, :] = last
    nh_ref[...] = last


def _short_conv(u, cb, hist, conv_w, row_off, nb, t_len):
    c = u.shape[1]
    tm = _pick(t_len, 512)
    nt = t_len // tm
    off = row_off // tm
    rows = lambda b, t: (off + b * nt + t, 0)
    conv_pre, new_hist = pl.pallas_call(
        _conv_kernel,
        grid=(nb, nt),
        in_specs=[pl.BlockSpec((tm, c), rows),
                  pl.BlockSpec((tm, c), rows),
                  pl.BlockSpec((None, 2, c), lambda b, t: (b, 0, 0)),
                  pl.BlockSpec((3, c), lambda b, t: (0, 0))],
        out_specs=[pl.BlockSpec((tm, c), lambda b, t: (b * nt + t, 0)),
                   pl.BlockSpec((None, 2, c), lambda b, t: (b, 0, 0))],
        out_shape=[jax.ShapeDtypeStruct((nb * t_len, c), BF16),
                   jax.ShapeDtypeStruct((nb, 2, c), F32)],
        scratch_shapes=[pltpu.VMEM((8, c), F32)],
        compiler_params=_cparams(("arbitrary", "arbitrary")),
        name="short_conv",
    )(u, cb, hist, conv_w)
    return conv_pre, new_hist


def _sel_kernel(ki_ref, qit_ref, wit_ref, bias_ref, key_ref, pcut_ref, *,
                tq, ch, cc, lpad, l_valid, causal, ktop, idx_dim):
    i = pl.program_id(1)
    nch_all = lpad // ch
    if causal:
        nch = jnp.minimum(((i + 1) * tq + ch - 1) // ch, nch_all)
    else:
        nch = nch_all
    scale = idx_dim ** -0.5
    qit = qit_ref[...]
    w = wit_ref[...]
    qh = [qit[h * idx_dim:(h + 1) * idx_dim, :] for h in range(IDX_HEADS)]
    if causal:
        q_pos = i * tq + lax.broadcasted_iota(I32, (1, tq), 1)
        k_lim = jnp.minimum((q_pos // CHUNK + 1) * CHUNK, l_valid)
    else:
        k_lim = l_valid
    row_iota = lax.broadcasted_iota(I32, (ch, tq), 0)
    row_iota_cc = lax.broadcasted_iota(I32, (cc, tq), 0)

    def score_chunk(c, carry):
        k0 = pl.multiple_of(c * ch, ch)
        kc = ki_ref[pl.ds(k0, ch), :]
        acc = jnp.zeros((ch, tq), F32)
        for h in range(IDX_HEADS):
            d = jnp.dot(kc, qh[h], preferred_element_type=F32)
            acc = acc + jnp.maximum(d, 0.0) * w[h:h + 1, :]
        key_ref[pl.ds(k0, ch), :] = jnp.where((k0 + row_iota) < k_lim, acc * scale, -jnp.inf)
        return carry

    lax.fori_loop(0, nch, score_chunk, 0)

    ncc = (nch * ch + cc - 1) // cc

    def neg_chunk(c, carry):
        k0 = pl.multiple_of(c * ch, ch)
        key_ref[pl.ds(k0, ch), :] = jnp.full((ch, tq), -jnp.inf, F32)
        return carry

    lax.fori_loop(nch, ncc * (cc // ch), neg_chunk, 0)

    def count(pred):
        def body(c, cnt):
            k0 = pl.multiple_of(c * cc, cc)
            blk = key_ref[pl.ds(k0, cc), :]
            ones = jnp.where(pred(blk, k0), 1.0, 0.0)
            return cnt + jnp.sum(ones.reshape(cc // COUNT_ROWS, COUNT_ROWS, tq), axis=0)
        cnt = lax.fori_loop(0, ncc, body, jnp.zeros((COUNT_ROWS, tq), F32))
        return jnp.sum(cnt, axis=0, keepdims=True)

    def as_score(key):
        return lax.bitcast_convert_type(jnp.where(key < 0, key ^ 0x7FFFFFFF, key), F32)

    kf = float(ktop)
    n_adm = jnp.broadcast_to(jnp.asarray(k_lim, F32), (1, tq))
    all_adm = n_adm <= kf
    c0 = count(lambda blk, k0: blk >= 0.0)
    cand0 = jnp.where(c0 >= kf, 0, INT_MIN).astype(I32)
    cnt0 = jnp.where(c0 >= kf, c0, kf + 1.0)

    def settled(cnt):
        return jnp.min(jnp.where(jnp.logical_or(all_adm, cnt == kf), 1.0, 0.0)) > 0.0

    def bit_cond(st):
        bit, _, cnt = st
        return jnp.logical_and(bit >= 0, jnp.logical_not(settled(cnt)))

    def bit_body(st):
        bit, cand, cnt = st
        trial = cand | lax.shift_left(jnp.int32(1), bit)
        trial_f = as_score(trial)
        c = count(lambda blk, k0: blk >= trial_f)
        take = c >= kf
        return bit - 1, jnp.where(take, trial, cand), jnp.where(take, c, cnt)

    _, cand, cnt = lax.while_loop(bit_cond, bit_body, (jnp.int32(30), cand0, cnt0))
    thr = jnp.where(all_adm, -jnp.inf, as_score(cand))
    pcut0 = jnp.where(all_adm, 0, lpad).astype(I32)
    pcut_ref[...] = pcut0

    @pl.when(jnp.logical_not(settled(cnt)))
    def _():
        tie = jnp.logical_not(jnp.logical_or(all_adm, cnt == kf))
        need = kf - count(lambda blk, k0: blk > thr)
        nbits = max(1, (lpad - 1).bit_length())

        def pbody(it, q):
            trial = q | lax.shift_left(jnp.int32(1), nbits - 1 - it)
            c = count(lambda blk, k0: jnp.logical_and(blk == thr, (k0 + row_iota_cc) < trial))
            return jnp.where(c < need, trial, q)

        q = lax.fori_loop(0, nbits, pbody, jnp.zeros((1, tq), I32))
        pcut_ref[...] = jnp.where(tie, q + 1, pcut0)

    pcut = pcut_ref[...]

    def write_chunk(c, carry):
        k0 = pl.multiple_of(c * ch, ch)
        blk = key_ref[pl.ds(k0, ch), :]
        sel = jnp.logical_or(blk > thr, jnp.logical_and(blk == thr, (k0 + row_iota) < pcut))
        bias_ref[pl.ds(k0, ch), :] = jnp.where(sel, 0.0, MASK_NEG).astype(bias_ref.dtype)
        return carry

    lax.fori_loop(0, nch, write_chunk, 0)

    def fill_chunk(c, carry):
        k0 = pl.multiple_of(c * ch, ch)
        bias_ref[pl.ds(k0, ch), :] = jnp.full((ch, tq), MASK_NEG, bias_ref.dtype)
        return carry

    lax.fori_loop(nch, nch_all, fill_chunk, 0)


def _select(ki, qit, wit, *, l_valid, causal, ktop, idx_dim, t_len=None):
    nb, lpad, _ = ki.shape
    t_len = qit.shape[2] if t_len is None else t_len
    tq = _pick(t_len, 256)
    ch = _pick(lpad, 512)
    cc = next(c for c in (4 * ch, 3 * ch, 2 * ch, ch) if lpad % c == 0)
    assert ch >= ktop and lpad % ch == 0 and ch % COUNT_ROWS == 0
    kern = functools.partial(_sel_kernel, tq=tq, ch=ch, cc=cc, lpad=lpad, l_valid=l_valid,
                             causal=causal, ktop=ktop, idx_dim=idx_dim)
    return pl.pallas_call(
        kern,
        grid=(nb, t_len // tq),
        in_specs=[pl.BlockSpec((None, lpad, idx_dim), lambda b, i: (b, 0, 0)),
                  pl.BlockSpec((None, qit.shape[1], tq), lambda b, i: (b, 0, i)),
                  pl.BlockSpec((None, wit.shape[1], tq), lambda b, i: (b, 0, i))],
        out_specs=pl.BlockSpec((None, lpad, tq), lambda b, i: (b, 0, i)),
        out_shape=jax.ShapeDtypeStruct((nb, lpad, t_len), BF16),
        scratch_shapes=[pltpu.VMEM((lpad, tq), F32), pltpu.VMEM((1, tq), I32)],
        compiler_params=_cparams(("parallel", "arbitrary")),
        name="index_select",
    )(ki, qit, wit)


def _att_kernel(it_ref, jt_ref, fin_ref, q_ref, k_ref, vt_ref, b_ref, o_ref, m_ref, acc_ref, s_ref, p_ref,
                bf_ref, *, tq, tk, n_kv, hd, rep):
    step = pl.program_id(1)

    @pl.when(jt_ref[step] == 0)
    def _():
        m_ref[...] = jnp.full(m_ref.shape, MASK_NEG, F32)
        acc_ref[...] = jnp.zeros(acc_ref.shape, F32)

    def compute():
        r = rep * tq
        nblk = tk // ATT_ROWS
        bf_ref[...] = b_ref[...].astype(F32)

        def masked_scores(g, r0):
            mask = bf_ref[pl.ds(r0, ATT_ROWS), :]
            return s_ref[g, pl.ds(r0, ATT_ROWS), :] + jnp.concatenate([mask] * rep, axis=1)

        def scores(g):
            s_ref[g] = jnp.dot(k_ref[:, g * hd:(g + 1) * hd], q_ref[g], preferred_element_type=F32)

        def softmax_update(g):
            def max_body(b, m8):
                blk = masked_scores(g, pl.multiple_of(b * ATT_ROWS, ATT_ROWS))
                return jnp.maximum(m8, jnp.max(blk.reshape(ATT_ROWS // 8, 8, r), axis=0))

            m8 = lax.fori_loop(0, nblk, max_body, jnp.full((8, r), MASK_NEG, F32), unroll=True)
            m_prev = m_ref[g]
            m_new = jnp.maximum(m_prev, jnp.max(m8, axis=0, keepdims=True))
            alpha = jnp.exp2(m_prev - m_new)

            def exp_body(b, c):
                r0 = pl.multiple_of(b * ATT_ROWS, ATT_ROWS)
                p_ref[g, pl.ds(r0, ATT_ROWS), :] = jnp.exp2(masked_scores(g, r0) - m_new).astype(BF16)
                return c

            lax.fori_loop(0, nblk, exp_body, 0, unroll=True)
            m_ref[g] = m_new
            return alpha

        def weighted_values(g, alpha):
            vt_ones = jnp.concatenate([vt_ref[g * hd:(g + 1) * hd, :], jnp.ones((DEN_ROWS, tk), BF16)], axis=0)
            pv = jnp.dot(vt_ones, p_ref[g], preferred_element_type=F32)
            acc_ref[g] = alpha * acc_ref[g] + pv

        scores(0)
        for g in range(n_kv):
            if g + 1 < n_kv:
                scores(g + 1)
            weighted_values(g, softmax_update(g))

    compute()

    @pl.when(fin_ref[step] == 1)
    def _():
        for g in range(n_kv):
            o_t = (acc_ref[g, :hd, :] / acc_ref[g, hd:hd + 1, :]).T
            for rr in range(rep):
                h = g * rep + rr
                o_ref[:, h * hd:(h + 1) * hd] = o_t[rr * tq:(rr + 1) * tq, :].astype(o_ref.dtype)


def _attention(qt, k, vt, bias_t, *, tq, causal, nt=None):
    nb, nt_all, n_kv, hd, r = qt.shape
    nt = nt_all if nt is None else nt
    rep = r // tq
    lpad = k.shape[1]
    tk = _pick(lpad, 512)
    nk = lpad // tk
    n_keys = [min(nk, ((i + 1) * tq - 1) // tk + 1) if causal else nk for i in range(nt)]
    it = jnp.asarray([i for i in range(nt) for _ in range(n_keys[i])], I32)
    jt = jnp.asarray([j for i in range(nt) for j in range(n_keys[i])], I32)
    fin = jnp.asarray([int(j == n_keys[i] - 1) for i in range(nt) for j in range(n_keys[i])], I32)
    kern = functools.partial(_att_kernel, tq=tq, tk=tk, n_kv=n_kv, hd=hd, rep=rep)
    grid_spec = pltpu.PrefetchScalarGridSpec(
        num_scalar_prefetch=3,
        grid=(nb, int(it.shape[0])),
        in_specs=[pl.BlockSpec((None, None, n_kv, hd, r), lambda b, s, it, jt, fin: (b, it[s], 0, 0, 0)),
                  pl.BlockSpec((None, tk, n_kv * hd), lambda b, s, it, jt, fin: (b, jt[s], 0)),
                  pl.BlockSpec((None, n_kv * hd, tk), lambda b, s, it, jt, fin: (b, 0, jt[s])),
                  pl.BlockSpec((None, tk, tq), lambda b, s, it, jt, fin: (b, jt[s], it[s]))],
        out_specs=pl.BlockSpec((tq, n_kv * rep * hd), lambda b, s, it, jt, fin: (b * nt + it[s], 0)),
        scratch_shapes=[pltpu.VMEM((n_kv, 1, r), F32),
                        pltpu.VMEM((n_kv, hd + DEN_ROWS, r), F32),
                        pltpu.VMEM((n_kv, tk, r), F32),
                        pltpu.VMEM((n_kv, tk, r), BF16),
                        pltpu.VMEM((tk, tq), F32)],
    )
    return pl.pallas_call(
        kern,
        grid_spec=grid_spec,
        out_shape=jax.ShapeDtypeStruct((nb * nt * tq, n_kv * rep * hd), BF16),
        compiler_params=_cparams(("parallel", "arbitrary")),
        name="sparse_attention",
    )(it, jt, fin, qt, k, vt, bias_t)


def _mematt_kernel(q_ref, k_ref, v_ref, o_ref, *, n_heads, hd):
    scale = hd ** -0.5
    for h in range(n_heads):
        q = q_ref[:, h * hd:(h + 1) * hd]
        k = k_ref[:, h * hd:(h + 1) * hd].astype(BF16)
        v = v_ref[:, h * hd:(h + 1) * hd].astype(BF16)
        s = lax.dot_general(q, k, (((1,), (1,)), ((), ())), preferred_element_type=F32) * scale
        m = jnp.max(s, axis=-1, keepdims=True)
        p = jnp.exp(s - m)
        l = jnp.sum(p, axis=-1, keepdims=True)
        o = jnp.dot(p.astype(BF16), v, preferred_element_type=F32) / l
        o_ref[:, h * hd:(h + 1) * hd] = o.astype(o_ref.dtype)


def _mem_attention(mq, mk, mv, *, row_off, nb, t_len, n_heads, hd):
    tq = _pick(t_len, 512)
    nt = t_len // tq
    off = row_off // tq
    n_mem = mk.shape[1]
    w = n_heads * hd
    return pl.pallas_call(
        functools.partial(_mematt_kernel, n_heads=n_heads, hd=hd),
        grid=(nb, nt),
        in_specs=[pl.BlockSpec((tq, w), lambda b, i: (off + b * nt + i, 0)),
                  pl.BlockSpec((None, n_mem, w), lambda b, i: (b, 0, 0)),
                  pl.BlockSpec((None, n_mem, w), lambda b, i: (b, 0, 0))],
        out_specs=pl.BlockSpec((tq, w), lambda b, i: (b * nt + i, 0)),
        out_shape=jax.ShapeDtypeStruct((nb * t_len, w), BF16),
        compiler_params=_cparams(("parallel", "parallel")),
        name="memory_attention",
    )(mq, mk, mv)


def _router_kernel(x_ref, g_ref, w_ref, b_ref, xn_ref, idx_ref, p_ref):
    x = x_ref[...]
    ms = jnp.mean(x * x, axis=-1, keepdims=True)
    xn = x * lax.rsqrt(ms + EPS) * g_ref[...]
    xn_ref[...] = xn
    logits = jnp.dot(xn.astype(BF16), w_ref[...], preferred_element_type=F32) + b_ref[...]
    lane = lax.broadcasted_iota(I32, logits.shape, 1)
    lane_f = lane.astype(F32)
    vals, idxs = [], []
    l = logits
    for _ in range(TOP_K):
        m = jnp.max(l, axis=-1, keepdims=True)
        ix = jnp.min(jnp.where(l == m, lane_f, float(LANES)), axis=-1, keepdims=True)
        vals.append(m)
        idxs.append(ix)
        l = jnp.where(lane_f == ix, -jnp.inf, l)
    es = [jnp.exp(v - vals[0]) for v in vals]
    den = es[0]
    for e in es[1:]:
        den = den + e
    idx_out = jnp.zeros(logits.shape, F32)
    p_out = jnp.zeros(logits.shape, F32)
    for r in range(TOP_K):
        idx_out = jnp.where(lane == r, idxs[r], idx_out)
        p_out = jnp.where(lane == r, es[r] / den, p_out)
    idx_ref[...] = idx_out.astype(I32)
    p_ref[...] = p_out


def _router(x1, g, w_router, b_router):
    n, d = x1.shape
    ne = w_router.shape[1]
    tm = _pick(n, 512)
    w_pad = jnp.zeros((d, LANES), BF16).at[:, :ne].set(w_router.astype(BF16))
    b_pad = jnp.full((1, LANES), MASK_NEG, F32).at[0, :ne].set(b_router)
    return pl.pallas_call(
        _router_kernel,
        grid=(n // tm,),
        in_specs=[pl.BlockSpec((tm, d), lambda i: (i, 0)),
                  pl.BlockSpec((1, d), lambda i: (0, 0)),
                  pl.BlockSpec((d, LANES), lambda i: (0, 0)),
                  pl.BlockSpec((1, LANES), lambda i: (0, 0))],
        out_specs=[pl.BlockSpec((tm, d), lambda i: (i, 0)),
                   pl.BlockSpec((tm, LANES), lambda i: (i, 0)),
                   pl.BlockSpec((tm, LANES), lambda i: (i, 0))],
        out_shape=[jax.ShapeDtypeStruct((n, d), F32),
                   jax.ShapeDtypeStruct((n, LANES), I32),
                   jax.ShapeDtypeStruct((n, LANES), F32)],
        compiler_params=_cparams(("parallel",)),
        name="router",
    )(x1, g.reshape(1, d), w_pad, b_pad)


def _row_copy(src, dst, src_row, dst_row, sem):
    return pltpu.make_async_copy(src.at[pl.ds(src_row, 1)], dst.at[pl.ds(dst_row, 1)], sem)


def _for_rows(n_rows, row_fn):
    n_blocks = n_rows // ROW_UNROLL

    def block(b, c):
        for u in range(ROW_UNROLL):
            row_fn(b * ROW_UNROLL + u)
        return c

    lax.fori_loop(0, n_blocks, block, 0)

    def tail(r, c):
        row_fn(r)
        return c

    lax.fori_loop(n_blocks * ROW_UNROLL, n_rows, tail, 0)


def _moe_kernel(te_ref, tp_ref, tr_ref, nu_ref, ord_ref, x_hbm, wg_ref, wu_ref, bg_ref, bu_ref, wd_ref, bd_ref,
                y_hbm, xf_ref, xb_ref, acc_ref, gsem, ssem, *, tm, nf):
    t = pl.program_id(0)
    f = pl.program_id(1)
    shift = TOP_K.bit_length() - 1
    n_tok = x_hbm.shape[0]

    def scatter(tile, op):
        base = tp_ref[tile]

        def row(r):
            a = ord_ref[base + r]
            dst = (a & (TOP_K - 1)) * n_tok + lax.shift_right_logical(a, shift)
            cp = _row_copy(acc_ref, y_hbm, r, dst, ssem)
            cp.start() if op == "start" else cp.wait()

        _for_rows(tr_ref[tile], row)

    def gather(op):
        base = tp_ref[t]

        def row(r):
            cp = _row_copy(x_hbm, xf_ref, lax.shift_right_logical(ord_ref[base + r], shift), r, gsem)
            cp.start() if op == "start" else cp.wait()

        _for_rows(tm, row)

    @pl.when(t < nu_ref[0])
    def _():
        @pl.when(f == 0)
        def _():
            gather("start")
            gather("wait")
            xb_ref[...] = xf_ref[...].astype(BF16)

        xb = xb_ref[...]
        hg = jnp.dot(xb, wg_ref[...].astype(BF16), preferred_element_type=F32) + bg_ref[...]
        hu = jnp.dot(xb, wu_ref[...].astype(BF16), preferred_element_type=F32) + bu_ref[...]
        gate = jnp.minimum(hg, SWIGLU_LIMIT)
        up = jnp.clip(hu, -SWIGLU_LIMIT, SWIGLU_LIMIT)
        act = (up + 1.0) * gate * jax.nn.sigmoid(SWIGLU_ALPHA * gate)
        contrib = jnp.dot(act.astype(BF16), wd_ref[...].astype(BF16), preferred_element_type=F32)

        @pl.when(f == 0)
        def _():
            @pl.when(t > 0)
            def _():
                scatter(t - 1, "wait")

            acc_ref[...] = contrib + bd_ref[...]

        @pl.when(f > 0)
        def _():
            acc_ref[...] += contrib

        @pl.when(f == nf - 1)
        def _():
            scatter(t, "start")

            @pl.when(t == nu_ref[0] - 1)
            def _():
                scatter(t, "wait")


def _moe_experts(xn, tile_expert, tile_pos, tile_rows, n_used, order, w_up_gate, b_up_gate, w_down, b_down, *, tm, fc):
    n, d = xn.shape
    ne, _, ff2 = w_up_gate.shape
    ff = ff2 // 2
    nf = ff // fc
    mt = tile_expert.shape[0]

    def fe(t, f, nu):
        return jnp.where(t < nu[0], f, nf - 1)

    grid_spec = pltpu.PrefetchScalarGridSpec(
        num_scalar_prefetch=5,
        grid=(mt, nf),
        in_specs=[
            pl.BlockSpec(memory_space=pl.ANY),
            pl.BlockSpec((None, d, fc), lambda t, f, te, tp, tr, nu, o: (te[t], 0, fe(t, f, nu))),
            pl.BlockSpec((None, d, fc), lambda t, f, te, tp, tr, nu, o: (te[t], 0, nf + fe(t, f, nu))),
            pl.BlockSpec((None, 1, fc), lambda t, f, te, tp, tr, nu, o: (te[t], 0, fe(t, f, nu))),
            pl.BlockSpec((None, 1, fc), lambda t, f, te, tp, tr, nu, o: (te[t], 0, nf + fe(t, f, nu))),
            pl.BlockSpec((None, fc, d), lambda t, f, te, tp, tr, nu, o: (te[t], fe(t, f, nu), 0)),
            pl.BlockSpec((None, 1, d), lambda t, f, te, tp, tr, nu, o: (te[t], 0, 0)),
        ],
        out_specs=pl.BlockSpec(memory_space=pl.ANY),
        scratch_shapes=[pltpu.VMEM((tm, d), F32), pltpu.VMEM((tm, d), BF16), pltpu.VMEM((tm, d), F32),
                        pltpu.SemaphoreType.DMA(()), pltpu.SemaphoreType.DMA(())],
    )
    return pl.pallas_call(
        functools.partial(_moe_kernel, tm=tm, nf=nf),
        grid_spec=grid_spec,
        out_shape=jax.ShapeDtypeStruct((n * TOP_K, d), F32),
        compiler_params=_cparams(("arbitrary", "arbitrary")),
        name="moe_experts",
    )(tile_expert, tile_pos, tile_rows, n_used, order, xn, w_up_gate, w_up_gate,
      b_up_gate.reshape(ne, 1, ff2), b_up_gate.reshape(ne, 1, ff2), w_down, b_down.reshape(ne, 1, d))


def _combine_kernel(x_ref, p_ref, g_ref, *refs):
    y_refs, o_ref = refs[:TOP_K], refs[TOP_K]
    p = p_ref[...]
    x = x_ref[...]
    for k in range(TOP_K):
        x = x + p[:, k:k + 1] * y_refs[k][...]
    ms = jnp.mean(x * x, axis=-1, keepdims=True)
    o_ref[...] = x * lax.rsqrt(ms + EPS) * g_ref[...]


def _combine(x1, probs, y, g_final):
    n, d = x1.shape
    tm = _pick(n, 256)
    nt = n // tm
    y_specs = [pl.BlockSpec((tm, d), functools.partial(lambda i, k: (k * nt + i, 0), k=k)) for k in range(TOP_K)]
    return pl.pallas_call(
        _combine_kernel,
        grid=(nt,),
        in_specs=[pl.BlockSpec((tm, d), lambda i: (i, 0)),
                  pl.BlockSpec((tm, LANES), lambda i: (i, 0)),
                  pl.BlockSpec((1, d), lambda i: (0, 0))] + y_specs,
        out_specs=pl.BlockSpec((tm, d), lambda i: (i, 0)),
        out_shape=jax.ShapeDtypeStruct((n, d), F32),
        compiler_params=_cparams(("parallel",)),
        name="moe_combine",
    )(x1, probs, g_final.reshape(1, d), *([y] * TOP_K))


def _rope_tables(pos, dim):
    half = dim // 2
    inv_freq = ROPE_THETA ** (-jnp.arange(half, dtype=F32) / half)
    ang = pos.astype(F32)[:, None] * inv_freq[None, :]
    cos, sin = jnp.cos(ang), jnp.sin(ang)
    reps = LANES // dim
    cos_t = jnp.tile(jnp.concatenate([cos, cos], axis=-1), (1, reps))
    sin_t = jnp.tile(jnp.concatenate([-sin, sin], axis=-1), (1, reps))
    return cos_t, sin_t


def _pad_len(l, mult):
    return (l + mult - 1) // mult * mult


def kernel(x_prompt, x_sample, cache_attn_k, cache_attn_v, cache_idx_k, state_conv, cache_mem_k, cache_mem_v, mem_prompt, g_mix, w_comb, conv_w, w_conv_proj, w_attn_proj, g_mem, w_mem_kv, w_mem_proj, w_merge_out, g_moe, w_router, b_router, w_up_gate, b_up_gate, w_down, b_down, g_final):
    bp, t_p, d = x_prompt.shape
    bs, t_s, _ = x_sample.shape
    assert bp == 1 and g_mix.shape[0] == 1
    past = cache_attn_k.shape[2]
    n_kv, hd = cache_attn_k.shape[3], cache_attn_k.shape[4]
    idx_dim = cache_idx_k.shape[3]
    cw = conv_w.shape[2]
    attn_q = w_attn_proj.shape[1]
    n_heads = attn_q // hd
    attn_kv = n_kv * hd
    idx_q = IDX_HEADS * idx_dim
    n_mem, mem_heads, mem_hd = cache_mem_k.shape[2], cache_mem_k.shape[3], cache_mem_k.shape[4]
    mem_q = mem_heads * mem_hd
    n_p, n_s = bp * t_p, bs * t_s
    n = n_p + n_s

    o_cin, o_cb, o_cc = 0, cw, 2 * cw
    o_q = 3 * cw
    o_k = o_q + attn_q
    o_v = o_k + attn_kv
    o_qi = o_v + attn_kv
    o_ki = o_qi + idx_q
    o_wi = o_ki + idx_dim
    o_mq = o_wi + IDX_HEADS
    o_g = o_mq + mem_q
    wc = w_comb[0]
    w_main = wc[:, :o_ki].astype(BF16)
    w_kw = jnp.zeros((d, LANES), BF16).at[:, :idx_dim + IDX_HEADS].set(wc[:, o_ki:o_mq].astype(BF16))
    w_mq = wc[:, o_mq:o_g].astype(BF16)
    w_gates = wc[:, o_g:].astype(BF16)

    x = jnp.concatenate([x_prompt.reshape(n_p, d), x_sample.reshape(n_s, d)], axis=0)
    pos = jnp.concatenate([jnp.arange(t_p, dtype=I32),
                           jnp.tile(past + jnp.arange(t_s, dtype=I32), bs)])
    cos_h, sin_h = _rope_tables(pos, hd)
    cos_i, sin_i = _rope_tables(pos, idx_dim)
    tm = _pick(n, 512)
    rowtab = lambda a: (a, (tm, LANES), lambda i, j: (i, 0))

    xn = _rmsnorm(x, g_mix[0], BF16)

    tn_c = _pick(cw, 512)
    nbc = cw // tn_c

    def ep_conv(accs, extras, outs):
        outs[0][...] = accs[2] * accs[0]
        outs[1][...] = accs[1]

    u, cb = _mm(xn, [(w_main, 0), (w_main, nbc), (w_main, 2 * nbc)], ep_conv,
                [(cw, F32, tn_c, None), (cw, F32, tn_c, None)], tn=tn_c, name="proj_conv")

    attn_scale = hd ** -0.5 * LOG2E
    tn_q = _pick(attn_q, 1024)

    rep = n_heads // n_kv
    tq_p = _pick(t_p, 256)
    fused_t = tm % tq_p == 0 and n_p % tm == 0 and t_p % LANES == 0 and tn_q % (rep * hd) == 0
    qpt = tm // tq_p
    gpc = tn_q // (rep * hd)

    def ep_q(accs, extras, outs):
        pieces = _rope_groups(accs[0], extras[0][...], extras[1][...], hd // 2)
        for c, p in enumerate(pieces):
            p = p * attn_scale
            outs[0][:, c * LANES:(c + 1) * LANES] = p.astype(BF16)
            if fused_t:
                p_t = p.T
                g_loc, rr = divmod(c, rep)
                for t in range(qpt):
                    outs[1][t, g_loc, :, rr * tq_p:(rr + 1) * tq_p] = p_t[:, t * tq_p:(t + 1) * tq_p].astype(BF16)

    q_outs = [(attn_q, BF16, tn_q, None)]
    if fused_t:
        q_outs.append(((n // tq_p, n_kv, hd, rep * tq_p), BF16, (qpt, gpc, hd, rep * tq_p),
                       lambda i, j: (i, j, 0, 0), None))
    q_res = _mm(xn, [(w_main, o_q // tn_q)], ep_q, q_outs,
                extras=[rowtab(cos_h), rowtab(sin_h)], tn=tn_q, name="proj_q")
    q_r = q_res[0]

    def ep_k(accs, extras, outs):
        pieces = _rope_groups(accs[0], extras[0][...], extras[1][...], hd // 2)
        for c, p in enumerate(pieces):
            outs[0][:, c * LANES:(c + 1) * LANES] = p
            outs[1][:, c * LANES:(c + 1) * LANES] = p.astype(BF16)

    k_f, k_b = _mm(xn, [(w_main, o_k // attn_kv)], ep_k,
                   [(attn_kv, F32, attn_kv, None), (attn_kv, BF16, attn_kv, None)],
                   extras=[rowtab(cos_h), rowtab(sin_h)], tn=attn_kv, name="proj_k")

    def ep_v(accs, extras, outs):
        outs[0][...] = accs[0]
        outs[1][...] = accs[0].astype(BF16)
        if fused_t:
            outs[2][...] = accs[0].T.astype(BF16)

    v_outs = [(attn_kv, F32, attn_kv, None), (attn_kv, BF16, attn_kv, None)]
    if fused_t:
        v_outs.append(((attn_kv, n), BF16, (attn_kv, tm), lambda i, j: (0, i), None))
    v_res = _mm(xn, [(w_main, o_v // attn_kv)], ep_v, v_outs, tn=attn_kv, name="proj_v")
    v_f, v_b = v_res[0], v_res[1]

    tn_i = _pick(idx_q, 1024)

    fused_i = fused_t and tn_i == idx_q

    def ep_qi(accs, extras, outs):
        pieces = _rope_groups(accs[0], extras[0][...], extras[1][...], idx_dim // 2)
        for c, p in enumerate(pieces):
            outs[0][:, c * LANES:(c + 1) * LANES] = p.astype(BF16)
            if fused_i:
                outs[1][c * LANES:(c + 1) * LANES, :] = p.T.astype(BF16)

    qi_outs = [(idx_q, BF16, tn_i, None)]
    if fused_i:
        qi_outs.append(((idx_q, n), BF16, (idx_q, tm), lambda i, j: (0, i), None))
    qi_res = _mm(xn, [(w_main, o_qi // tn_i)], ep_qi, qi_outs,
                 extras=[rowtab(cos_i), rowtab(sin_i)], tn=tn_i, name="proj_qi")
    qi_r = qi_res[0]

    wi_scale = IDX_HEADS ** -0.5

    def ep_kw(accs, extras, outs):
        a = accs[0]
        (roped,) = _rope_groups(a, extras[0][...], extras[1][...], idx_dim // 2)
        lane = lax.broadcasted_iota(I32, a.shape, 1)
        outs[0][...] = jnp.where(lane < idx_dim, roped, a * wi_scale)

    (kw,) = _mm(xn, [(w_kw, 0)], ep_kw, [(LANES, F32, LANES, None)],
                extras=[rowtab(cos_i), rowtab(sin_i)], tn=LANES, name="proj_ki_wi")
    ki_f = kw[:, :idx_dim]
    wi = kw[:, idx_dim:idx_dim + IDX_HEADS]

    def ep_cast(accs, extras, outs):
        outs[0][...] = accs[0].astype(outs[0].dtype)

    tn_m = _pick(mem_q, 1024)
    (mq,) = _mm(xn, [(w_mq, 0)], ep_cast, [(mem_q, BF16, tn_m, None)], tn=tn_m, name="proj_mq")

    def ep_sig(accs, extras, outs):
        outs[0][...] = jax.nn.sigmoid(accs[0])

    tn_g = _pick(d, 1024)
    (sg,) = _mm(xn, [(w_gates, 0)], ep_sig, [(3 * d, F32, tn_g, None)], tn=tn_g, name="proj_gates")

    conv_p, hist_p = _short_conv(u, cb, jnp.zeros((bp, 2, cw), F32), conv_w[0], 0, bp, t_p)
    conv_s, hist_s = _short_conv(u, cb, state_conv[0], conv_w[0], n_p, bs, t_s)
    conv_pre = jnp.concatenate([conv_p, conv_s], axis=0)

    ki_b = ki_f.astype(BF16)

    def to_qt(q, nb, t_len, tq):
        q = q.reshape(nb, t_len // tq, tq, n_kv, rep, hd)
        return jnp.transpose(q, (0, 1, 3, 5, 4, 2)).reshape(nb, t_len // tq, n_kv, hd, rep * tq)

    l_p = t_p
    lpad_p = _pad_len(l_p, LANES)
    pad_p = lpad_p - l_p
    qit_p = qi_res[1][None] if fused_i else qi_r[:n_p].T[None]
    bias_p = _select(jnp.pad(ki_b[:n_p], ((0, pad_p), (0, 0)))[None], qit_p, wi.T[None], t_len=t_p,
                     l_valid=l_p, causal=True, ktop=min(TOPK_MAX, l_p // 4), idx_dim=idx_dim)
    kp = jnp.pad(k_b[:n_p], ((0, pad_p), (0, 0)))[None]
    if fused_t:
        qt_p, vtp = q_res[1][None], v_res[2][None]
    else:
        qt_p, vtp = to_qt(q_r[:n_p], bp, t_p, tq_p), jnp.pad(v_b[:n_p].T, ((0, 0), (0, pad_p)))[None]
    attn_p = _attention(qt_p, kp, vtp, bias_p, tq=tq_p, causal=True, nt=t_p // tq_p)

    l_s = past + t_s
    lpad_s = _pad_len(l_s, 512)
    pad_s = lpad_s - l_s
    ki_all = jnp.concatenate([cache_idx_k[0].astype(BF16), ki_b[n_p:].reshape(bs, t_s, idx_dim)], axis=1)
    bias_s = _select(jnp.pad(ki_all, ((0, 0), (0, pad_s), (0, 0))),
                     jnp.swapaxes(qi_r[n_p:].reshape(bs, t_s, idx_q), 1, 2),
                     jnp.swapaxes(wi[n_p:].reshape(bs, t_s, IDX_HEADS), 1, 2),
                     l_valid=l_s, causal=False, ktop=min(TOPK_MAX, l_s // 4), idx_dim=idx_dim)
    ks = jnp.concatenate([cache_attn_k[0].reshape(bs, past, attn_kv).astype(BF16),
                          k_b[n_p:].reshape(bs, t_s, attn_kv)], axis=1)
    vs = jnp.concatenate([cache_attn_v[0].reshape(bs, past, attn_kv).astype(BF16),
                          v_b[n_p:].reshape(bs, t_s, attn_kv)], axis=1)
    ks = jnp.pad(ks, ((0, 0), (0, pad_s), (0, 0)))
    vts = jnp.pad(jnp.swapaxes(vs, 1, 2), ((0, 0), (0, 0), (0, pad_s)))
    attn_s = _attention(to_qt(q_r[n_p:], bs, t_s, t_s), ks, vts, bias_s, tq=t_s, causal=False)
    attn = jnp.concatenate([attn_p, attn_s], axis=0)

    memn = _rmsnorm(mem_prompt.reshape(bp * n_mem, d), g_mem[0], BF16)
    tn_kv = _pick(2 * mem_q, 1024)

    def ep_f32(accs, extras, outs):
        outs[0][...] = accs[0]

    (mem_kv,) = _mm(memn, [(w_mem_kv[0].astype(BF16), 0)], ep_f32, [(2 * mem_q, F32, tn_kv, None)],
                    tn=tn_kv, name="proj_mem_kv")
    mk_p = mem_kv[:, :mem_q].reshape(bp, n_mem, mem_q)
    mv_p = mem_kv[:, mem_q:].reshape(bp, n_mem, mem_q)
    memo_p = _mem_attention(mq, mk_p, mv_p, row_off=0, nb=bp, t_len=t_p, n_heads=mem_heads, hd=mem_hd)
    memo_s = _mem_attention(mq, cache_mem_k[0].reshape(bs, n_mem, mem_q), cache_mem_v[0].reshape(bs, n_mem, mem_q),
                            row_off=n_p, nb=bs, t_len=t_s, n_heads=mem_heads, hd=mem_hd)
    memo = jnp.concatenate([memo_p, memo_s], axis=0)

    ngb = d // tn_g

    def gate_extra(which):
        return (sg, (tm, tn_g), functools.partial(lambda i, j, o: (i, o + j), o=which * ngb))

    def ep_gate0(accs, extras, outs):
        outs[0][...] = extras[0][...] * accs[0]

    def ep_gate(accs, extras, outs):
        outs[0][...] = (extras[1][...] + extras[0][...] * accs[0]).astype(outs[0].dtype)

    prev = lambda a: (a, (tm, tn_g), lambda i, j: (i, j))
    (m1,) = _mm(conv_pre, [(w_conv_proj[0].astype(BF16), 0)], ep_gate0, [(d, F32, tn_g, None)],
                extras=[gate_extra(0)], tn=tn_g, name="merge_conv")
    (m2,) = _mm(attn, [(w_attn_proj[0].astype(BF16), 0)], ep_gate, [(d, F32, tn_g, None)],
                extras=[gate_extra(1), prev(m1)], tn=tn_g, name="merge_attn")
    (m3,) = _mm(memo, [(w_mem_proj[0].astype(BF16), 0)], ep_gate, [(d, BF16, tn_g, None)],
                extras=[gate_extra(2), prev(m2)], tn=tn_g, name="merge_mem")

    def ep_res(accs, extras, outs):
        outs[0][...] = extras[0][...] + accs[0]

    (x1,) = _mm(m3, [(w_merge_out[0].astype(BF16), 0)], ep_res, [(d, F32, tn_g, None)],
                extras=[prev(x)], tn=tn_g, name="merge_out")

    xn2, ridx, rp = _router(x1, g_moe[0], w_router[0], b_router[0])
    ne = w_router.shape[2]
    ff = w_down.shape[2]
    nk = n * TOP_K
    tme = MOE_TILE if nk >= 16 * MOE_TILE else 128
    e_flat = ridx[:, :TOP_K].reshape(nk)
    order = jnp.argsort(e_flat, stable=True).astype(I32)
    experts = jnp.arange(ne, dtype=I32)
    counts = jnp.sum((e_flat[:, None] == experts[None, :]).astype(I32), axis=0)
    tiles_per = (counts + tme - 1) // tme
    tile_end = jnp.cumsum(tiles_per)
    tile_first = tile_end - tiles_per
    cstart = jnp.cumsum(counts) - counts
    n_used = tile_end[-1].astype(I32)
    mt = nk // tme + ne
    tile_ids = jnp.arange(mt, dtype=I32)
    owner = jnp.logical_and(tile_ids[:, None] >= tile_first[None, :], tile_ids[:, None] < tile_end[None, :])
    pick = lambda v: jnp.sum(jnp.where(owner, v[None, :], 0), axis=1).astype(I32)
    k_in = tile_ids - pick(tile_first)
    tile_pos = pick(cstart) + k_in * tme
    tile_rows = jnp.clip(pick(counts) - k_in * tme, 0, tme)
    last_e = jnp.sum(jnp.where(tile_ids == n_used - 1, pick(experts), 0))
    tile_expert = jnp.where(tile_ids < n_used, pick(experts), last_e).astype(I32)
    order_pad = jnp.concatenate([order, jnp.zeros((tme,), I32)])
    y_assign = _moe_experts(xn2, tile_expert, tile_pos, tile_rows, n_used.reshape(1), order_pad,
                            w_up_gate[0], b_up_gate[0], w_down[0], b_down[0], tm=tme, fc=_pick(ff, 256))
    y = _combine(x1, rp, y_assign, g_final)

    y_prompt = y[:n_p].reshape(bp, t_p, d)
    y_sample = y[n_p:].reshape(bs, t_s, d)
    new_k_p = k_f[:n_p].reshape(1, bp, t_p, n_kv, hd)
    new_v_p = v_f[:n_p].reshape(1, bp, t_p, n_kv, hd)
    new_ki_p = ki_f[:n_p].reshape(1, bp, t_p, idx_dim)
    new_k_s = k_f[n_p:].reshape(1, bs, t_s, n_kv, hd)
    new_v_s = v_f[n_p:].reshape(1, bs, t_s, n_kv, hd)
    new_ki_s = ki_f[n_p:].reshape(1, bs, t_s, idx_dim)
    return (y_prompt, y_sample, new_k_p, new_v_p, new_ki_p, hist_p[None],
            mk_p.reshape(1, bp, n_mem, mem_heads, mem_hd), mv_p.reshape(1, bp, n_mem, mem_heads, mem_hd),
            new_k_s, new_v_s, new_ki_s, hist_s[None])
```

```python
import functools

import jax
import jax.numpy as jnp
from jax import lax
from jax.experimental import pallas as pl
from jax.experimental.pallas import tpu as pltpu

F32 = jnp.float32
BF16 = jnp.bfloat16
I32 = jnp.int32

CHUNK = 64
IDX_HEADS = 16
TOPK_MAX = 256
TOP_K = 4
SWIGLU_LIMIT = 7.0
SWIGLU_ALPHA = 1.702
ROPE_THETA = 10000.0
EPS = 1e-6

LANES = 128
VMEM_LIMIT = 56 * 1024 * 1024
LOG2E = 1.4426950408889634
DEN_ROWS = 16
ROW_UNROLL = 8
ATT_ROWS = 32
COUNT_ROWS = 64
MOE_TILE = 1024
MASK_NEG = -1e30
INT_MIN = -2147483648
NEGINF_KEY = -2139095041


def _cparams(sem):
    return pltpu.CompilerParams(dimension_semantics=sem, vmem_limit_bytes=VMEM_LIMIT)


def _pick(n, pref):
    if n <= pref:
        return n
    t = pref
    while n % t:
        t //= 2
    return t


def _rms_kernel(x_ref, g_ref, o_ref):
    x = x_ref[...]
    ms = jnp.mean(x * x, axis=-1, keepdims=True)
    o_ref[...] = (x * lax.rsqrt(ms + EPS) * g_ref[...]).astype(o_ref.dtype)


def _rmsnorm(x, g, out_dtype):
    n, d = x.shape
    tm = _pick(n, 512)
    return pl.pallas_call(
        _rms_kernel,
        grid=(n // tm,),
        in_specs=[pl.BlockSpec((tm, d), lambda i: (i, 0)),
                  pl.BlockSpec((1, d), lambda i: (0, 0))],
        out_specs=pl.BlockSpec((tm, d), lambda i: (i, 0)),
        out_shape=jax.ShapeDtypeStruct((n, d), out_dtype),
        compiler_params=_cparams(("parallel",)),
        name="rmsnorm",
    )(x, g.reshape(1, d))


def _mm_kernel(epilogue, n_w, n_extra, x_ref, *refs):
    w_refs = refs[:n_w]
    extras = refs[n_w:n_w + n_extra]
    outs = refs[n_w + n_extra:]
    x = x_ref[...]
    accs = [jnp.dot(x, w[...], preferred_element_type=F32) for w in w_refs]
    epilogue(accs, extras, outs)


def _mm(x, ws, epilogue, outs, extras=(), *, tn, tm=512, name="mm"):
    n, k = x.shape
    tm = _pick(n, tm)
    ncol = outs[0][0] // outs[0][2]
    in_specs = [pl.BlockSpec((tm, k), lambda i, j: (i, 0))]
    args = [x]
    for w, off in ws:
        in_specs.append(pl.BlockSpec((k, tn), functools.partial(lambda i, j, o: (0, o + j), o=off)))
        args.append(w)
    for a, bs, im in extras:
        in_specs.append(pl.BlockSpec(bs, im))
        args.append(a)
    out_specs, out_shape = [], []
    for o in outs:
        if len(o) == 5:
            shape, dt, bs, im, _ = o
            out_specs.append(pl.BlockSpec(bs, im))
            out_shape.append(jax.ShapeDtypeStruct(shape, dt))
            continue
        width, dt, bw, im = o
        out_specs.append(pl.BlockSpec((tm, bw), im if im is not None else (lambda i, j: (i, j))))
        out_shape.append(jax.ShapeDtypeStruct((n, width), dt))
    res = pl.pallas_call(
        functools.partial(_mm_kernel, epilogue, len(ws), len(extras)),
        grid=(n // tm, ncol),
        in_specs=in_specs,
        out_specs=out_specs,
        out_shape=out_shape,
        compiler_params=_cparams(("parallel", "arbitrary")),
        name=name,
    )(*args)
    return res


def _rope_groups(acc, cos, sin, half):
    tn = acc.shape[1]
    lane = lax.broadcasted_iota(I32, (acc.shape[0], LANES), 1)
    pieces = []
    for c in range(tn // LANES):
        a = acc[:, c * LANES:(c + 1) * LANES]
        if 2 * half == LANES:
            partner = pltpu.roll(a, half, axis=1)
        else:
            first = (lane & (2 * half - 1)) < half
            partner = jnp.where(first, pltpu.roll(a, LANES - half, axis=1), pltpu.roll(a, half, axis=1))
        pieces.append(a * cos + partner * sin)
    return pieces


def _conv_kernel(u_ref, cb_ref, h_ref, w_ref, o_ref, nh_ref, carry_ref):
    t = pl.program_id(1)

    @pl.when(t == 0)
    def _():
        carry_ref[0:2, :] = h_ref[...]

    u = u_ref[...]
    tm = u.shape[0]
    h0 = carry_ref[0:1, :]
    h1 = carry_ref[1:2, :]
    row = lax.broadcasted_iota(I32, u.shape, 0)
    p1 = jnp.where(row == 0, h1, pltpu.roll(u, 1, axis=0))
    p2 = jnp.where(row == 0, h0, jnp.where(row == 1, h1, pltpu.roll(u, 2, axis=0)))
    w = w_ref[...]
    y = w[0:1, :] * p2 + w[1:2, :] * p1 + w[2:3, :] * u
    o_ref[...] = (cb_ref[...] * y).astype(o_ref.dtype)
    last = u[tm - 2:tm, :]
    carry_ref[0:2, :] = last
    nh_ref[...] = last


def _short_conv(u, cb, hist, conv_w, row_off, nb, t_len):
    c = u.shape[1]
    tm = _pick(t_len, 512)
    nt = t_len // tm
    off = row_off // tm
    rows = lambda b, t: (off + b * nt + t, 0)
    conv_pre, new_hist = pl.pallas_call(
        _conv_kernel,
        grid=(nb, nt),
        in_specs=[pl.BlockSpec((tm, c), rows),
                  pl.BlockSpec((tm, c), rows),
                  pl.BlockSpec((None, 2, c), lambda b, t: (b, 0, 0)),
                  pl.BlockSpec((3, c), lambda b, t: (0, 0))],
        out_specs=[pl.BlockSpec((tm, c), lambda b, t: (b * nt + t, 0)),
                   pl.BlockSpec((None, 2, c), lambda b, t: (b, 0, 0))],
        out_shape=[jax.ShapeDtypeStruct((nb * t_len, c), BF16),
                   jax.ShapeDtypeStruct((nb, 2, c), F32)],
        scratch_shapes=[pltpu.VMEM((8, c), F32)],
        compiler_params=_cparams(("arbitrary", "arbitrary")),
        name="short_conv",
    )(u, cb, hist, conv_w)
    return conv_pre, new_hist


def _sel_kernel(ki_ref, qit_ref, wit_ref, bias_ref, key_ref, pcut_ref, *,
                tq, ch, cc, lpad, l_valid, causal, ktop, idx_dim):
    i = pl.program_id(1)
    nch_all = lpad // ch
    if causal:
        nch = jnp.minimum(((i + 1) * tq + ch - 1) // ch, nch_all)
    else:
        nch = nch_all
    scale = idx_dim ** -0.5
    qit = qit_ref[...]
    w = wit_ref[...]
    qh = [qit[h * idx_dim:(h + 1) * idx_dim, :] for h in range(IDX_HEADS)]
    if causal:
        q_pos = i * tq + lax.broadcasted_iota(I32, (1, tq), 1)
        k_lim = jnp.minimum((q_pos // CHUNK + 1) * CHUNK, l_valid)
    else:
        k_lim = l_valid
    row_iota = lax.broadcasted_iota(I32, (ch, tq), 0)
    row_iota_cc = lax.broadcasted_iota(I32, (cc, tq), 0)

    def score_chunk(c, carry):
        k0 = pl.multiple_of(c * ch, ch)
        kc = ki_ref[pl.ds(k0, ch), :]
        acc = jnp.zeros((ch, tq), F32)
        for h in range(IDX_HEADS):
            d = jnp.dot(kc, qh[h], preferred_element_type=F32)
            acc = acc + jnp.maximum(d, 0.0) * w[h:h + 1, :]
        key_ref[pl.ds(k0, ch), :] = jnp.where((k0 + row_iota) < k_lim, acc * scale, -jnp.inf)
        return carry

    lax.fori_loop(0, nch, score_chunk, 0)

    ncc = (nch * ch + cc - 1) // cc

    def neg_chunk(c, carry):
        k0 = pl.multiple_of(c * ch, ch)
        key_ref[pl.ds(k0, ch), :] = jnp.full((ch, tq), -jnp.inf, F32)
        return carry

    lax.fori_loop(nch, ncc * (cc // ch), neg_chunk, 0)

    def count(pred):
        def body(c, cnt):
            k0 = pl.multiple_of(c * cc, cc)
            blk = key_ref[pl.ds(k0, cc), :]
            ones = jnp.where(pred(blk, k0), 1.0, 0.0)
            return cnt + jnp.sum(ones.reshape(cc // COUNT_ROWS, COUNT_ROWS, tq), axis=0)
        cnt = lax.fori_loop(0, ncc, body, jnp.zeros((COUNT_ROWS, tq), F32))
        return jnp.sum(cnt, axis=0, keepdims=True)

    def as_score(key):
        return lax.bitcast_convert_type(jnp.where(key < 0, key ^ 0x7FFFFFFF, key), F32)

    kf = float(ktop)
    n_adm = jnp.broadcast_to(jnp.asarray(k_lim, F32), (1, tq))
    all_adm = n_adm <= kf
    c0 = count(lambda blk, k0: blk >= 0.0)
    cand0 = jnp.where(c0 >= kf, 0, INT_MIN).astype(I32)
    cnt0 = jnp.where(c0 >= kf, c0, kf + 1.0)

    def settled(cnt):
        return jnp.min(jnp.where(jnp.logical_or(all_adm, cnt == kf), 1.0, 0.0)) > 0.0

    def bit_cond(st):
        bit, _, cnt = st
        return jnp.logical_and(bit >= 0, jnp.logical_not(settled(cnt)))

    def bit_body(st):
        bit, cand, cnt = st
        trial = cand | lax.shift_left(jnp.int32(1), bit)
        trial_f = as_score(trial)
        c = count(lambda blk, k0: blk >= trial_f)
        take = c >= kf
        return bit - 1, jnp.where(take, trial, cand), jnp.where(take, c, cnt)

    _, cand, cnt = lax.while_loop(bit_cond, bit_body, (jnp.int32(30), cand0, cnt0))
    thr = jnp.where(all_adm, -jnp.inf, as_score(cand))
    pcut0 = jnp.where(all_adm, 0, lpad).astype(I32)
    pcut_ref[...] = pcut0

    @pl.when(jnp.logical_not(settled(cnt)))
    def _():
        tie = jnp.logical_not(jnp.logical_or(all_adm, cnt == kf))
        need = kf - count(lambda blk, k0: blk > thr)
        nbits = max(1, (lpad - 1).bit_length())

        def pbody(it, q):
            trial = q | lax.shift_left(jnp.int32(1), nbits - 1 - it)
            c = count(lambda blk, k0: jnp.logical_and(blk == thr, (k0 + row_iota_cc) < trial))
            return jnp.where(c < need, trial, q)

        q = lax.fori_loop(0, nbits, pbody, jnp.zeros((1, tq), I32))
        pcut_ref[...] = jnp.where(tie, q + 1, pcut0)

    pcut = pcut_ref[...]

    def write_chunk(c, carry):
        k0 = pl.multiple_of(c * ch, ch)
        blk = key_ref[pl.ds(k0, ch), :]
        sel = jnp.logical_or(blk > thr, jnp.logical_and(blk == thr, (k0 + row_iota) < pcut))
        bias_ref[pl.ds(k0, ch), :] = jnp.where(sel, 0.0, MASK_NEG).astype(bias_ref.dtype)
        return carry

    lax.fori_loop(0, nch, write_chunk, 0)

    def fill_chunk(c, carry):
        k0 = pl.multiple_of(c * ch, ch)
        bias_ref[pl.ds(k0, ch), :] = jnp.full((ch, tq), MASK_NEG, bias_ref.dtype)
        return carry

    lax.fori_loop(nch, nch_all, fill_chunk, 0)


def _select(ki, qit, wit, *, l_valid, causal, ktop, idx_dim, t_len=None):
    nb, lpad, _ = ki.shape
    t_len = qit.shape[2] if t_len is None else t_len
    tq = _pick(t_len, 256)
    ch = _pick(lpad, 512)
    cc = next(c for c in (2 * ch, 3 * ch, ch) if lpad % c == 0)
    assert ch >= ktop and lpad % ch == 0 and ch % COUNT_ROWS == 0
    kern = functools.partial(_sel_kernel, tq=tq, ch=ch, cc=cc, lpad=lpad, l_valid=l_valid,
                             causal=causal, ktop=ktop, idx_dim=idx_dim)
    return pl.pallas_call(
        kern,
        grid=(nb, t_len // tq),
        in_specs=[pl.BlockSpec((None, lpad, idx_dim), lambda b, i: (b, 0, 0)),
                  pl.BlockSpec((None, qit.shape[1], tq), lambda b, i: (b, 0, i)),
                  pl.BlockSpec((None, wit.shape[1], tq), lambda b, i: (b, 0, i))],
        out_specs=pl.BlockSpec((None, lpad, tq), lambda b, i: (b, 0, i)),
        out_shape=jax.ShapeDtypeStruct((nb, lpad, t_len), BF16),
        scratch_shapes=[pltpu.VMEM((lpad, tq), F32), pltpu.VMEM((1, tq), I32)],
        compiler_params=_cparams(("parallel", "arbitrary")),
        name="index_select",
    )(ki, qit, wit)


def _att_kernel(it_ref, jt_ref, fin_ref, q_ref, k_ref, vt_ref, b_ref, o_ref, m_ref, acc_ref, s_ref, p_ref,
                bf_ref, *, tq, tk, n_kv, hd, rep):
    step = pl.program_id(1)

    @pl.when(jt_ref[step] == 0)
    def _():
        m_ref[...] = jnp.full(m_ref.shape, MASK_NEG, F32)
        acc_ref[...] = jnp.zeros(acc_ref.shape, F32)

    def compute():
        r = rep * tq
        nblk = tk // ATT_ROWS
        bf_ref[...] = b_ref[...].astype(F32)

        def masked_scores(g, r0):
            mask = bf_ref[pl.ds(r0, ATT_ROWS), :]
            return s_ref[g, pl.ds(r0, ATT_ROWS), :] + jnp.concatenate([mask] * rep, axis=1)

        def scores(g):
            s_ref[g] = jnp.dot(k_ref[:, g * hd:(g + 1) * hd], q_ref[g], preferred_element_type=F32)

        def softmax_update(g):
            def max_body(b, m8):
                blk = masked_scores(g, pl.multiple_of(b * ATT_ROWS, ATT_ROWS))
                return jnp.maximum(m8, jnp.max(blk.reshape(ATT_ROWS // 8, 8, r), axis=0))

            m8 = lax.fori_loop(0, nblk, max_body, jnp.full((8, r), MASK_NEG, F32), unroll=True)
            m_prev = m_ref[g]
            m_new = jnp.maximum(m_prev, jnp.max(m8, axis=0, keepdims=True))
            alpha = jnp.exp2(m_prev - m_new)

            def exp_body(b, c):
                r0 = pl.multiple_of(b * ATT_ROWS, ATT_ROWS)
                p_ref[g, pl.ds(r0, ATT_ROWS), :] = jnp.exp2(masked_scores(g, r0) - m_new).astype(BF16)
                return c

            lax.fori_loop(0, nblk, exp_body, 0, unroll=True)
            m_ref[g] = m_new
            return alpha

        def weighted_values(g, alpha):
            vt_ones = jnp.concatenate([vt_ref[g * hd:(g + 1) * hd, :], jnp.ones((DEN_ROWS, tk), BF16)], axis=0)
            pv = jnp.dot(vt_ones, p_ref[g], preferred_element_type=F32)
            acc_ref[g] = alpha * acc_ref[g] + pv

        scores(0)
        for g in range(n_kv):
            if g + 1 < n_kv:
                scores(g + 1)
            weighted_values(g, softmax_update(g))

    compute()

    @pl.when(fin_ref[step] == 1)
    def _():
        for g in range(n_kv):
            o_t = (acc_ref[g, :hd, :] / acc_ref[g, hd:hd + 1, :]).T
            for rr in range(rep):
                h = g * rep + rr
                o_ref[:, h * hd:(h + 1) * hd] = o_t[rr * tq:(rr + 1) * tq, :].astype(o_ref.dtype)


def _attention(qt, k, vt, bias_t, *, tq, causal, nt=None):
    nb, nt_all, n_kv, hd, r = qt.shape
    nt = nt_all if nt is None else nt
    rep = r // tq
    lpad = k.shape[1]
    tk = _pick(lpad, 512)
    nk = lpad // tk
    n_keys = [min(nk, ((i + 1) * tq - 1) // tk + 1) if causal else nk for i in range(nt)]
    it = jnp.asarray([i for i in range(nt) for _ in range(n_keys[i])], I32)
    jt = jnp.asarray([j for i in range(nt) for j in range(n_keys[i])], I32)
    fin = jnp.asarray([int(j == n_keys[i] - 1) for i in range(nt) for j in range(n_keys[i])], I32)
    kern = functools.partial(_att_kernel, tq=tq, tk=tk, n_kv=n_kv, hd=hd, rep=rep)
    grid_spec = pltpu.PrefetchScalarGridSpec(
        num_scalar_prefetch=3,
        grid=(nb, int(it.shape[0])),
        in_specs=[pl.BlockSpec((None, None, n_kv, hd, r), lambda b, s, it, jt, fin: (b, it[s], 0, 0, 0)),
                  pl.BlockSpec((None, tk, n_kv * hd), lambda b, s, it, jt, fin: (b, jt[s], 0)),
                  pl.BlockSpec((None, n_kv * hd, tk), lambda b, s, it, jt, fin: (b, 0, jt[s])),
                  pl.BlockSpec((None, tk, tq), lambda b, s, it, jt, fin: (b, jt[s], it[s]))],
        out_specs=pl.BlockSpec((tq, n_kv * rep * hd), lambda b, s, it, jt, fin: (b * nt + it[s], 0)),
        scratch_shapes=[pltpu.VMEM((n_kv, 1, r), F32),
                        pltpu.VMEM((n_kv, hd + DEN_ROWS, r), F32),
                        pltpu.VMEM((n_kv, tk, r), F32),
                        pltpu.VMEM((n_kv, tk, r), BF16),
                        pltpu.VMEM((tk, tq), F32)],
    )
    return pl.pallas_call(
        kern,
        grid_spec=grid_spec,
        out_shape=jax.ShapeDtypeStruct((nb * nt * tq, n_kv * rep * hd), BF16),
        compiler_params=_cparams(("parallel", "arbitrary")),
        name="sparse_attention",
    )(it, jt, fin, qt, k, vt, bias_t)


def _mematt_kernel(q_ref, k_ref, v_ref, o_ref, *, n_heads, hd):
    scale = hd ** -0.5
    for h in range(n_heads):
        q = q_ref[:, h * hd:(h + 1) * hd]
        k = k_ref[:, h * hd:(h + 1) * hd].astype(BF16)
        v = v_ref[:, h * hd:(h + 1) * hd].astype(BF16)
        s = lax.dot_general(q, k, (((1,), (1,)), ((), ())), preferred_element_type=F32) * scale
        m = jnp.max(s, axis=-1, keepdims=True)
        p = jnp.exp(s - m)
        l = jnp.sum(p, axis=-1, keepdims=True)
        o = jnp.dot(p.astype(BF16), v, preferred_element_type=F32) / l
        o_ref[:, h * hd:(h + 1) * hd] = o.astype(o_ref.dtype)


def _mem_attention(mq, mk, mv, *, row_off, nb, t_len, n_heads, hd):
    tq = _pick(t_len, 512)
    nt = t_len // tq
    off = row_off // tq
    n_mem = mk.shape[1]
    w = n_heads * hd
    return pl.pallas_call(
        functools.partial(_mematt_kernel, n_heads=n_heads, hd=hd),
        grid=(nb, nt),
        in_specs=[pl.BlockSpec((tq, w), lambda b, i: (off + b * nt + i, 0)),
                  pl.BlockSpec((None, n_mem, w), lambda b, i: (b, 0, 0)),
                  pl.BlockSpec((None, n_mem, w), lambda b, i: (b, 0, 0))],
        out_specs=pl.BlockSpec((tq, w), lambda b, i: (b * nt + i, 0)),
        out_shape=jax.ShapeDtypeStruct((nb * t_len, w), BF16),
        compiler_params=_cparams(("parallel", "parallel")),
        name="memory_attention",
    )(mq, mk, mv)


def _router_kernel(x_ref, g_ref, w_ref, b_ref, xn_ref, idx_ref, p_ref):
    x = x_ref[...]
    ms = jnp.mean(x * x, axis=-1, keepdims=True)
    xn = x * lax.rsqrt(ms + EPS) * g_ref[...]
    xn_ref[...] = xn
    logits = jnp.dot(xn.astype(BF16), w_ref[...], preferred_element_type=F32) + b_ref[...]
    lane = lax.broadcasted_iota(I32, logits.shape, 1)
    lane_f = lane.astype(F32)
    vals, idxs = [], []
    l = logits
    for _ in range(TOP_K):
        m = jnp.max(l, axis=-1, keepdims=True)
        ix = jnp.min(jnp.where(l == m, lane_f, float(LANES)), axis=-1, keepdims=True)
        vals.append(m)
        idxs.append(ix)
        l = jnp.where(lane_f == ix, -jnp.inf, l)
    es = [jnp.exp(v - vals[0]) for v in vals]
    den = es[0]
    for e in es[1:]:
        den = den + e
    idx_out = jnp.zeros(logits.shape, F32)
    p_out = jnp.zeros(logits.shape, F32)
    for r in range(TOP_K):
        idx_out = jnp.where(lane == r, idxs[r], idx_out)
        p_out = jnp.where(lane == r, es[r] / den, p_out)
    idx_ref[...] = idx_out.astype(I32)
    p_ref[...] = p_out


def _router(x1, g, w_router, b_router):
    n, d = x1.shape
    ne = w_router.shape[1]
    tm = _pick(n, 512)
    w_pad = jnp.zeros((d, LANES), BF16).at[:, :ne].set(w_router.astype(BF16))
    b_pad = jnp.full((1, LANES), MASK_NEG, F32).at[0, :ne].set(b_router)
    return pl.pallas_call(
        _router_kernel,
        grid=(n // tm,),
        in_specs=[pl.BlockSpec((tm, d), lambda i: (i, 0)),
                  pl.BlockSpec((1, d), lambda i: (0, 0)),
                  pl.BlockSpec((d, LANES), lambda i: (0, 0)),
                  pl.BlockSpec((1, LANES), lambda i: (0, 0))],
        out_specs=[pl.BlockSpec((tm, d), lambda i: (i, 0)),
                   pl.BlockSpec((tm, LANES), lambda i: (i, 0)),
                   pl.BlockSpec((tm, LANES), lambda i: (i, 0))],
        out_shape=[jax.ShapeDtypeStruct((n, d), F32),
                   jax.ShapeDtypeStruct((n, LANES), I32),
                   jax.ShapeDtypeStruct((n, LANES), F32)],
        compiler_params=_cparams(("parallel",)),
        name="router",
    )(x1, g.reshape(1, d), w_pad, b_pad)


def _row_copy(src, dst, src_row, dst_row, sem):
    return pltpu.make_async_copy(src.at[pl.ds(src_row, 1)], dst.at[pl.ds(dst_row, 1)], sem)


def _for_rows(n_rows, row_fn):
    n_blocks = n_rows // ROW_UNROLL

    def block(b, c):
        for u in range(ROW_UNROLL):
            row_fn(b * ROW_UNROLL + u)
        return c

    lax.fori_loop(0, n_blocks, block, 0)

    def tail(r, c):
        row_fn(r)
        return c

    lax.fori_loop(n_blocks * ROW_UNROLL, n_rows, tail, 0)


def _moe_kernel(te_ref, tp_ref, tr_ref, nu_ref, ord_ref, x_hbm, wg_ref, wu_ref, bg_ref, bu_ref, wd_ref, bd_ref,
                y_hbm, xf_ref, xb_ref, acc_ref, gsem, ssem, *, tm, nf):
    t = pl.program_id(0)
    f = pl.program_id(1)
    shift = TOP_K.bit_length() - 1
    n_tok = x_hbm.shape[0]

    def scatter(tile, op):
        base = tp_ref[tile]

        def row(r):
            a = ord_ref[base + r]
            dst = (a & (TOP_K - 1)) * n_tok + lax.shift_right_logical(a, shift)
            cp = _row_copy(acc_ref, y_hbm, r, dst, ssem)
            cp.start() if op == "start" else cp.wait()

        _for_rows(tr_ref[tile], row)

    def gather(op):
        base = tp_ref[t]

        def row(r):
            cp = _row_copy(x_hbm, xf_ref, lax.shift_right_logical(ord_ref[base + r], shift), r, gsem)
            cp.start() if op == "start" else cp.wait()

        _for_rows(tm, row)

    @pl.when(t < nu_ref[0])
    def _():
        @pl.when(f == 0)
        def _():
            gather("start")
            gather("wait")
            xb_ref[...] = xf_ref[...].astype(BF16)

        xb = xb_ref[...]
        hg = jnp.dot(xb, wg_ref[...].astype(BF16), preferred_element_type=F32) + bg_ref[...]
        hu = jnp.dot(xb, wu_ref[...].astype(BF16), preferred_element_type=F32) + bu_ref[...]
        gate = jnp.minimum(hg, SWIGLU_LIMIT)
        up = jnp.clip(hu, -SWIGLU_LIMIT, SWIGLU_LIMIT)
        act = (up + 1.0) * gate * jax.nn.sigmoid(SWIGLU_ALPHA * gate)
        contrib = jnp.dot(act.astype(BF16), wd_ref[...].astype(BF16), preferred_element_type=F32)

        @pl.when(f == 0)
        def _():
            @pl.when(t > 0)
            def _():
                scatter(t - 1, "wait")

            acc_ref[...] = contrib + bd_ref[...]

        @pl.when(f > 0)
        def _():
            acc_ref[...] += contrib

        @pl.when(f == nf - 1)
        def _():
            scatter(t, "start")

            @pl.when(t == nu_ref[0] - 1)
            def _():
                scatter(t, "wait")


def _moe_experts(xn, tile_expert, tile_pos, tile_rows, n_used, order, w_up_gate, b_up_gate, w_down, b_down, *, tm, fc):
    n, d = xn.shape
    ne, _, ff2 = w_up_gate.shape
    ff = ff2 // 2
    nf = ff // fc
    mt = tile_expert.shape[0]

    def fe(t, f, nu):
        return jnp.where(t < nu[0], f, nf - 1)

    grid_spec = pltpu.PrefetchScalarGridSpec(
        num_scalar_prefetch=5,
        grid=(mt, nf),
        in_specs=[
            pl.BlockSpec(memory_space=pl.ANY),
            pl.BlockSpec((None, d, fc), lambda t, f, te, tp, tr, nu, o: (te[t], 0, fe(t, f, nu))),
            pl.BlockSpec((None, d, fc), lambda t, f, te, tp, tr, nu, o: (te[t], 0, nf + fe(t, f, nu))),
            pl.BlockSpec((None, 1, fc), lambda t, f, te, tp, tr, nu, o: (te[t], 0, fe(t, f, nu))),
            pl.BlockSpec((None, 1, fc), lambda t, f, te, tp, tr, nu, o: (te[t], 0, nf + fe(t, f, nu))),
            pl.BlockSpec((None, fc, d), lambda t, f, te, tp, tr, nu, o: (te[t], fe(t, f, nu), 0)),
            pl.BlockSpec((None, 1, d), lambda t, f, te, tp, tr, nu, o: (te[t], 0, 0)),
        ],
        out_specs=pl.BlockSpec(memory_space=pl.ANY),
        scratch_shapes=[pltpu.VMEM((tm, d), F32), pltpu.VMEM((tm, d), BF16), pltpu.VMEM((tm, d), F32),
                        pltpu.SemaphoreType.DMA(()), pltpu.SemaphoreType.DMA(())],
    )
    return pl.pallas_call(
        functools.partial(_moe_kernel, tm=tm, nf=nf),
        grid_spec=grid_spec,
        out_shape=jax.ShapeDtypeStruct((n * TOP_K, d), F32),
        compiler_params=_cparams(("arbitrary", "arbitrary")),
        name="moe_experts",
    )(tile_expert, tile_pos, tile_rows, n_used, order, xn, w_up_gate, w_up_gate,
      b_up_gate.reshape(ne, 1, ff2), b_up_gate.reshape(ne, 1, ff2), w_down, b_down.reshape(ne, 1, d))


def _combine_kernel(x_ref, p_ref, g_ref, *refs):
    y_refs, o_ref = refs[:TOP_K], refs[TOP_K]
    p = p_ref[...]
    x = x_ref[...]
    for k in range(TOP_K):
        x = x + p[:, k:k + 1] * y_refs[k][...]
    ms = jnp.mean(x * x, axis=-1, keepdims=True)
    o_ref[...] = x * lax.rsqrt(ms + EPS) * g_ref[...]


def _combine(x1, probs, y, g_final):
    n, d = x1.shape
    tm = _pick(n, 256)
    nt = n // tm
    y_specs = [pl.BlockSpec((tm, d), functools.partial(lambda i, k: (k * nt + i, 0), k=k)) for k in range(TOP_K)]
    return pl.pallas_call(
        _combine_kernel,
        grid=(nt,),
        in_specs=[pl.BlockSpec((tm, d), lambda i: (i, 0)),
                  pl.BlockSpec((tm, LANES), lambda i: (i, 0)),
                  pl.BlockSpec((1, d), lambda i: (0, 0))] + y_specs,
        out_specs=pl.BlockSpec((tm, d), lambda i: (i, 0)),
        out_shape=jax.ShapeDtypeStruct((n, d), F32),
        compiler_params=_cparams(("parallel",)),
        name="moe_combine",
    )(x1, probs, g_final.reshape(1, d), *([y] * TOP_K))


def _rope_tables(pos, dim):
    half = dim // 2
    inv_freq = ROPE_THETA ** (-jnp.arange(half, dtype=F32) / half)
    ang = pos.astype(F32)[:, None] * inv_freq[None, :]
    cos, sin = jnp.cos(ang), jnp.sin(ang)
    reps = LANES // dim
    cos_t = jnp.tile(jnp.concatenate([cos, cos], axis=-1), (1, reps))
    sin_t = jnp.tile(jnp.concatenate([-sin, sin], axis=-1), (1, reps))
    return cos_t, sin_t


def _pad_len(l, mult):
    return (l + mult - 1) // mult * mult


def kernel(x_prompt, x_sample, cache_attn_k, cache_attn_v, cache_idx_k, state_conv, cache_mem_k, cache_mem_v, mem_prompt, g_mix, w_comb, conv_w, w_conv_proj, w_attn_proj, g_mem, w_mem_kv, w_mem_proj, w_merge_out, g_moe, w_router, b_router, w_up_gate, b_up_gate, w_down, b_down, g_final):
    bp, t_p, d = x_prompt.shape
    bs, t_s, _ = x_sample.shape
    assert bp == 1 and g_mix.shape[0] == 1
    past = cache_attn_k.shape[2]
    n_kv, hd = cache_attn_k.shape[3], cache_attn_k.shape[4]
    idx_dim = cache_idx_k.shape[3]
    cw = conv_w.shape[2]
    attn_q = w_attn_proj.shape[1]
    n_heads = attn_q // hd
    attn_kv = n_kv * hd
    idx_q = IDX_HEADS * idx_dim
    n_mem, mem_heads, mem_hd = cache_mem_k.shape[2], cache_mem_k.shape[3], cache_mem_k.shape[4]
    mem_q = mem_heads * mem_hd
    n_p, n_s = bp * t_p, bs * t_s
    n = n_p + n_s

    o_cin, o_cb, o_cc = 0, cw, 2 * cw
    o_q = 3 * cw
    o_k = o_q + attn_q
    o_v = o_k + attn_kv
    o_qi = o_v + attn_kv
    o_ki = o_qi + idx_q
    o_wi = o_ki + idx_dim
    o_mq = o_wi + IDX_HEADS
    o_g = o_mq + mem_q
    wc = w_comb[0]
    w_main = wc[:, :o_ki].astype(BF16)
    w_kw = jnp.zeros((d, LANES), BF16).at[:, :idx_dim + IDX_HEADS].set(wc[:, o_ki:o_mq].astype(BF16))
    w_mq = wc[:, o_mq:o_g].astype(BF16)
    w_gates = wc[:, o_g:].astype(BF16)

    x = jnp.concatenate([x_prompt.reshape(n_p, d), x_sample.reshape(n_s, d)], axis=0)
    pos = jnp.concatenate([jnp.arange(t_p, dtype=I32),
                           jnp.tile(past + jnp.arange(t_s, dtype=I32), bs)])
    cos_h, sin_h = _rope_tables(pos, hd)
    cos_i, sin_i = _rope_tables(pos, idx_dim)
    tm = _pick(n, 512)
    rowtab = lambda a: (a, (tm, LANES), lambda i, j: (i, 0))

    xn = _rmsnorm(x, g_mix[0], BF16)

    tn_c = _pick(cw, 512)
    nbc = cw // tn_c

    def ep_conv(accs, extras, outs):
        outs[0][...] = accs[2] * accs[0]
        outs[1][...] = accs[1]

    u, cb = _mm(xn, [(w_main, 0), (w_main, nbc), (w_main, 2 * nbc)], ep_conv,
                [(cw, F32, tn_c, None), (cw, F32, tn_c, None)], tn=tn_c, name="proj_conv")

    attn_scale = hd ** -0.5 * LOG2E
    tn_q = _pick(attn_q, 1024)

    rep = n_heads // n_kv
    tq_p = _pick(t_p, 256)
    fused_t = tm % tq_p == 0 and n_p % tm == 0 and t_p % LANES == 0 and tn_q % (rep * hd) == 0
    qpt = tm // tq_p
    gpc = tn_q // (rep * hd)

    def ep_q(accs, extras, outs):
        pieces = _rope_groups(accs[0], extras[0][...], extras[1][...], hd // 2)
        for c, p in enumerate(pieces):
            p = p * attn_scale
            outs[0][:, c * LANES:(c + 1) * LANES] = p.astype(BF16)
            if fused_t:
                p_t = p.T
                g_loc, rr = divmod(c, rep)
                for t in range(qpt):
                    outs[1][t, g_loc, :, rr * tq_p:(rr + 1) * tq_p] = p_t[:, t * tq_p:(t + 1) * tq_p].astype(BF16)

    q_outs = [(attn_q, BF16, tn_q, None)]
    if fused_t:
        q_outs.append(((n // tq_p, n_kv, hd, rep * tq_p), BF16, (qpt, gpc, hd, rep * tq_p),
                       lambda i, j: (i, j, 0, 0), None))
    q_res = _mm(xn, [(w_main, o_q // tn_q)], ep_q, q_outs,
                extras=[rowtab(cos_h), rowtab(sin_h)], tn=tn_q, name="proj_q")
    q_r = q_res[0]

    def ep_k(accs, extras, outs):
        pieces = _rope_groups(accs[0], extras[0][...], extras[1][...], hd // 2)
        for c, p in enumerate(pieces):
            outs[0][:, c * LANES:(c + 1) * LANES] = p
            outs[1][:, c * LANES:(c + 1) * LANES] = p.astype(BF16)

    k_f, k_b = _mm(xn, [(w_main, o_k // attn_kv)], ep_k,
                   [(attn_kv, F32, attn_kv, None), (attn_kv, BF16, attn_kv, None)],
                   extras=[rowtab(cos_h), rowtab(sin_h)], tn=attn_kv, name="proj_k")

    def ep_v(accs, extras, outs):
        outs[0][...] = accs[0]
        outs[1][...] = accs[0].astype(BF16)
        if fused_t:
            outs[2][...] = accs[0].T.astype(BF16)

    v_outs = [(attn_kv, F32, attn_kv, None), (attn_kv, BF16, attn_kv, None)]
    if fused_t:
        v_outs.append(((attn_kv, n), BF16, (attn_kv, tm), lambda i, j: (0, i), None))
    v_res = _mm(xn, [(w_main, o_v // attn_kv)], ep_v, v_outs, tn=attn_kv, name="proj_v")
    v_f, v_b = v_res[0], v_res[1]

    tn_i = _pick(idx_q, 1024)

    fused_i = fused_t and tn_i == idx_q

    def ep_qi(accs, extras, outs):
        pieces = _rope_groups(accs[0], extras[0][...], extras[1][...], idx_dim // 2)
        for c, p in enumerate(pieces):
            outs[0][:, c * LANES:(c + 1) * LANES] = p.astype(BF16)
            if fused_i:
                outs[1][c * LANES:(c + 1) * LANES, :] = p.T.astype(BF16)

    qi_outs = [(idx_q, BF16, tn_i, None)]
    if fused_i:
        qi_outs.append(((idx_q, n), BF16, (idx_q, tm), lambda i, j: (0, i), None))
    qi_res = _mm(xn, [(w_main, o_qi // tn_i)], ep_qi, qi_outs,
                 extras=[rowtab(cos_i), rowtab(sin_i)], tn=tn_i, name="proj_qi")
    qi_r = qi_res[0]

    wi_scale = IDX_HEADS ** -0.5

    def ep_kw(accs, extras, outs):
        a = accs[0]
        (roped,) = _rope_groups(a, extras[0][...], extras[1][...], idx_dim // 2)
        lane = lax.broadcasted_iota(I32, a.shape, 1)
        outs[0][...] = jnp.where(lane < idx_dim, roped, a * wi_scale)

    (kw,) = _mm(xn, [(w_kw, 0)], ep_kw, [(LANES, F32, LANES, None)],
                extras=[rowtab(cos_i), rowtab(sin_i)], tn=LANES, name="proj_ki_wi")
    ki_f = kw[:, :idx_dim]
    wi = kw[:, idx_dim:idx_dim + IDX_HEADS]

    def ep_cast(accs, extras, outs):
        outs[0][...] = accs[0].astype(outs[0].dtype)

    tn_m = _pick(mem_q, 1024)
    (mq,) = _mm(xn, [(w_mq, 0)], ep_cast, [(mem_q, BF16, tn_m, None)], tn=tn_m, name="proj_mq")

    def ep_sig(accs, extras, outs):
        outs[0][...] = jax.nn.sigmoid(accs[0])

    tn_g = _pick(d, 1024)
    (sg,) = _mm(xn, [(w_gates, 0)], ep_sig, [(3 * d, F32, tn_g, None)], tn=tn_g, name="proj_gates")

    conv_p, hist_p = _short_conv(u, cb, jnp.zeros((bp, 2, cw), F32), conv_w[0], 0, bp, t_p)
    conv_s, hist_s = _short_conv(u, cb, state_conv[0], conv_w[0], n_p, bs, t_s)
    conv_pre = jnp.concatenate([conv_p, conv_s], axis=0)

    ki_b = ki_f.astype(BF16)

    def to_qt(q, nb, t_len, tq):
        q = q.reshape(nb, t_len // tq, tq, n_kv, rep, hd)
        return jnp.transpose(q, (0, 1, 3, 5, 4, 2)).reshape(nb, t_len // tq, n_kv, hd, rep * tq)

    l_p = t_p
    lpad_p = _pad_len(l_p, LANES)
    pad_p = lpad_p - l_p
    qit_p = qi_res[1][None] if fused_i else qi_r[:n_p].T[None]
    bias_p = _select(jnp.pad(ki_b[:n_p], ((0, pad_p), (0, 0)))[None], qit_p, wi.T[None], t_len=t_p,
                     l_valid=l_p, causal=True, ktop=min(TOPK_MAX, l_p // 4), idx_dim=idx_dim)
    kp = jnp.pad(k_b[:n_p], ((0, pad_p), (0, 0)))[None]
    if fused_t:
        qt_p, vtp = q_res[1][None], v_res[2][None]
    else:
        qt_p, vtp = to_qt(q_r[:n_p], bp, t_p, tq_p), jnp.pad(v_b[:n_p].T, ((0, 0), (0, pad_p)))[None]
    attn_p = _attention(qt_p, kp, vtp, bias_p, tq=tq_p, causal=True, nt=t_p // tq_p)

    l_s = past + t_s
    lpad_s = _pad_len(l_s, 512)
    pad_s = lpad_s - l_s
    ki_all = jnp.concatenate([cache_idx_k[0].astype(BF16), ki_b[n_p:].reshape(bs, t_s, idx_dim)], axis=1)
    bias_s = _select(jnp.pad(ki_all, ((0, 0), (0, pad_s), (0, 0))),
                     jnp.swapaxes(qi_r[n_p:].reshape(bs, t_s, idx_q), 1, 2),
                     jnp.swapaxes(wi[n_p:].reshape(bs, t_s, IDX_HEADS), 1, 2),
                     l_valid=l_s, causal=False, ktop=min(TOPK_MAX, l_s // 4), idx_dim=idx_dim)
    ks = jnp.concatenate([cache_attn_k[0].reshape(bs, past, attn_kv).astype(BF16),
                          k_b[n_p:].reshape(bs, t_s, attn_kv)], axis=1)
    vs = jnp.concatenate([cache_attn_v[0].reshape(bs, past, attn_kv).astype(BF16),
                          v_b[n_p:].reshape(bs, t_s, attn_kv)], axis=1)
    ks = jnp.pad(ks, ((0, 0), (0, pad_s), (0, 0)))
    vts = jnp.pad(jnp.swapaxes(vs, 1, 2), ((0, 0), (0, 0), (0, pad_s)))
    attn_s = _attention(to_qt(q_r[n_p:], bs, t_s, t_s), ks, vts, bias_s, tq=t_s, causal=False)
    attn = jnp.concatenate([attn_p, attn_s], axis=0)

    memn = _rmsnorm(mem_prompt.reshape(bp * n_mem, d), g_mem[0], BF16)
    tn_kv = _pick(2 * mem_q, 1024)

    def ep_f32(accs, extras, outs):
        outs[0][...] = accs[0]

    (mem_kv,) = _mm(memn, [(w_mem_kv[0].astype(BF16), 0)], ep_f32, [(2 * mem_q, F32, tn_kv, None)],
                    tn=tn_kv, name="proj_mem_kv")
    mk_p = mem_kv[:, :mem_q].reshape(bp, n_mem, mem_q)
    mv_p = mem_kv[:, mem_q:].reshape(bp, n_mem, mem_q)
    memo_p = _mem_attention(mq, mk_p, mv_p, row_off=0, nb=bp, t_len=t_p, n_heads=mem_heads, hd=mem_hd)
    memo_s = _mem_attention(mq, cache_mem_k[0].reshape(bs, n_mem, mem_q), cache_mem_v[0].reshape(bs, n_mem, mem_q),
                            row_off=n_p, nb=bs, t_len=t_s, n_heads=mem_heads, hd=mem_hd)
    memo = jnp.concatenate([memo_p, memo_s], axis=0)

    ngb = d // tn_g

    def gate_extra(which):
        return (sg, (tm, tn_g), functools.partial(lambda i, j, o: (i, o + j), o=which * ngb))

    def ep_gate0(accs, extras, outs):
        outs[0][...] = extras[0][...] * accs[0]

    def ep_gate(accs, extras, outs):
        outs[0][...] = (extras[1][...] + extras[0][...] * accs[0]).astype(outs[0].dtype)

    prev = lambda a: (a, (tm, tn_g), lambda i, j: (i, j))
    (m1,) = _mm(conv_pre, [(w_conv_proj[0].astype(BF16), 0)], ep_gate0, [(d, F32, tn_g, None)],
                extras=[gate_extra(0)], tn=tn_g, name="merge_conv")
    (m2,) = _mm(attn, [(w_attn_proj[0].astype(BF16), 0)], ep_gate, [(d, F32, tn_g, None)],
                extras=[gate_extra(1), prev(m1)], tn=tn_g, name="merge_attn")
    (m3,) = _mm(memo, [(w_mem_proj[0].astype(BF16), 0)], ep_gate, [(d, BF16, tn_g, None)],
                extras=[gate_extra(2), prev(m2)], tn=tn_g, name="merge_mem")

    def ep_res(accs, extras, outs):
        outs[0][...] = extras[0][...] + accs[0]

    (x1,) = _mm(m3, [(w_merge_out[0].astype(BF16), 0)], ep_res, [(d, F32, tn_g, None)],
                extras=[prev(x)], tn=tn_g, name="merge_out")

    xn2, ridx, rp = _router(x1, g_moe[0], w_router[0], b_router[0])
    ne = w_router.shape[2]
    ff = w_down.shape[2]
    nk = n * TOP_K
    tme = MOE_TILE if nk >= 16 * MOE_TILE else 128
    e_flat = ridx[:, :TOP_K].reshape(nk)
    order = jnp.argsort(e_flat, stable=True).astype(I32)
    experts = jnp.arange(ne, dtype=I32)
    counts = jnp.sum((e_flat[:, None] == experts[None, :]).astype(I32), axis=0)
    tiles_per = (counts + tme - 1) // tme
    tile_end = jnp.cumsum(tiles_per)
    tile_first = tile_end - tiles_per
    cstart = jnp.cumsum(counts) - counts
    n_used = tile_end[-1].astype(I32)
    mt = nk // tme + ne
    tile_ids = jnp.arange(mt, dtype=I32)
    owner = jnp.logical_and(tile_ids[:, None] >= tile_first[None, :], tile_ids[:, None] < tile_end[None, :])
    pick = lambda v: jnp.sum(jnp.where(owner, v[None, :], 0), axis=1).astype(I32)
    k_in = tile_ids - pick(tile_first)
    tile_pos = pick(cstart) + k_in * tme
    tile_rows = jnp.clip(pick(counts) - k_in * tme, 0, tme)
    last_e = jnp.sum(jnp.where(tile_ids == n_used - 1, pick(experts), 0))
    tile_expert = jnp.where(tile_ids < n_used, pick(experts), last_e).astype(I32)
    order_pad = jnp.concatenate([order, jnp.zeros((tme,), I32)])
    y_assign = _moe_experts(xn2, tile_expert, tile_pos, tile_rows, n_used.reshape(1), order_pad,
                            w_up_gate[0], b_up_gate[0], w_down[0], b_down[0], tm=tme, fc=_pick(ff, 256))
    y = _combine(x1, rp, y_assign, g_final)

    y_prompt = y[:n_p].reshape(bp, t_p, d)
    y_sample = y[n_p:].reshape(bs, t_s, d)
    new_k_p = k_f[:n_p].reshape(1, bp, t_p, n_kv, hd)
    new_v_p = v_f[:n_p].reshape(1, bp, t_p, n_kv, hd)
    new_ki_p = ki_f[:n_p].reshape(1, bp, t_p, idx_dim)
    new_k_s = k_f[n_p:].reshape(1, bs, t_s, n_kv, hd)
    new_v_s = v_f[n_p:].reshape(1, bs, t_s, n_kv, hd)
    new_ki_s = ki_f[n_p:].reshape(1, bs, t_s, idx_dim)
    return (y_prompt, y_sample, new_k_p, new_v_p, new_ki_p, hist_p[None],
            mk_p.reshape(1, bp, n_mem, mem_heads, mem_hd), mv_p.reshape(1, bp, n_mem, mem_heads, mem_hd),
            new_k_s, new_v_s, new_ki_s, hist_s[None])
```
